```python
import jax
import jax.numpy as jnp
from jax import lax

D_MODEL = 1024
BATCH = 2
SEQ = 8192
DEPTH = 4

ATTN_HEADS = 8
HEAD_DIM = 64
ATTN_WIDTH = ATTN_HEADS * HEAD_DIM
POOL_WINDOWS = (2, 4, 8, 16)
POOL_GROUPS = len(POOL_WINDOWS)
POOL_GROUP_DIM = 64
POOL_WIDTH = POOL_GROUPS * POOL_GROUP_DIM
HGRN_HEADS = 4
HGRN_HEAD_DIM = 64
HGRN_WIDTH = HGRN_HEADS * HGRN_HEAD_DIM
MIX_WIDTH = ATTN_WIDTH + POOL_WIDTH + HGRN_WIDTH
SPLIT_SIZES = (ATTN_WIDTH, ATTN_WIDTH, ATTN_WIDTH, POOL_WIDTH, HGRN_WIDTH, HGRN_WIDTH, HGRN_WIDTH, HGRN_WIDTH)
IN_WIDTH = sum(SPLIT_SIZES)
D_FF = 2816
MOBA_BLOCK = 256
MOBA_TOPK = 3
MOBA_Q_CHUNK = 64
HGRN_CHUNK = 64
ROPE_THETA = 10000.0
EPS = 1e-6
NEG_INF = -1e30
LB_FLOOR = 1e-20

kernel_name = 'hybrid_moba_pool_hgrn2_macaron'


def rms_norm(x, gain):
    x32 = x.astype(jnp.float32)
    y = x32 * lax.rsqrt(jnp.mean(x32 * x32, axis=-1, keepdims=True) + EPS)
    return (y * gain.astype(jnp.float32)).astype(x.dtype)


def swiglu(h, w_gate, w_up, w_down):
    return (jax.nn.silu(h @ w_gate) * (h @ w_up)) @ w_down


def apply_rope(x, pos):
    half = x.shape[-1] // 2
    inv_freq = ROPE_THETA ** (-jnp.arange(half, dtype=jnp.float32) / half)
    ang = pos.astype(jnp.float32)[:, None] * inv_freq[None, :]
    cos, sin = jnp.cos(ang), jnp.sin(ang)
    x32 = x.astype(jnp.float32)
    x1, x2 = x32[..., :half], x32[..., half:]
    return jnp.concatenate([x1 * cos - x2 * sin, x2 * cos + x1 * sin], axis=-1).astype(x.dtype)


def moba_attention(q, k, v):
    b, h, s, dh = q.shape
    nb = -(-s // MOBA_BLOCK)
    pad = nb * MOBA_BLOCK - s
    widths = ((0, 0), (0, 0), (0, pad), (0, 0))
    k_blocks = jnp.pad(k, widths).reshape(b, h, nb, MOBA_BLOCK, dh)
    v_blocks = jnp.pad(v, widths).reshape(b, h, nb, MOBA_BLOCK, dh)
    k_mean = jnp.mean(k_blocks.astype(jnp.float32), axis=3)
    n_sel = min(MOBA_TOPK, nb - 1)
    scale = dh ** -0.5
    n_chunks = s // MOBA_Q_CHUNK
    q_chunks = q.reshape(b, h, n_chunks, MOBA_Q_CHUNK, dh).transpose(2, 0, 1, 3, 4)
    b_idx = jnp.arange(b)[:, None, None, None]
    h_idx = jnp.arange(h)[None, :, None, None]
    key_off = jnp.arange(MOBA_BLOCK)
    q_off = jnp.arange(MOBA_Q_CHUNK)
    blk_ids = jnp.arange(nb)

    def chunk(args):
        q_c, c = args
        q_pos = c * MOBA_Q_CHUNK + q_off
        own = (c * MOBA_Q_CHUNK) // MOBA_BLOCK
        k_own = lax.dynamic_index_in_dim(k_blocks, own, axis=2, keepdims=False)
        v_own = lax.dynamic_index_in_dim(v_blocks, own, axis=2, keepdims=False)
        s_own = jnp.einsum('bhqd,bhkd->bhqk', q_c, k_own).astype(jnp.float32) * scale
        s_own = jnp.where(own * MOBA_BLOCK + key_off[None, :] <= q_pos[:, None], s_own, NEG_INF)
        if n_sel == 0:
            p = jax.nn.softmax(s_own, axis=-1).astype(v.dtype)
            return jnp.einsum('bhqk,bhkd->bhqd', p, v_own)
        gate = jnp.einsum('bhqd,bhnd->bhqn', q_c.astype(jnp.float32), k_mean)
        gate = jnp.where(blk_ids < own, gate, NEG_INF)
        _, sel = lax.top_k(gate, n_sel)
        sel_ok = jnp.arange(n_sel) < own
        k_sel = k_blocks[b_idx, h_idx, sel]
        v_sel = v_blocks[b_idx, h_idx, sel]
        s_sel = jnp.einsum('bhqd,bhqnkd->bhqnk', q_c, k_sel).astype(jnp.float32) * scale
        s_sel = jnp.where(sel_ok[:, None], s_sel, NEG_INF)
        n_keys = n_sel * MOBA_BLOCK
        scores = jnp.concatenate([s_sel.reshape(b, h, MOBA_Q_CHUNK, n_keys), s_own], axis=-1)
        p = jax.nn.softmax(scores, axis=-1).astype(v.dtype)
        p_sel = p[..., :n_keys].reshape(b, h, MOBA_Q_CHUNK, n_sel, MOBA_BLOCK)
        out = jnp.einsum('bhqnk,bhqnkd->bhqd', p_sel, v_sel)
        return out + jnp.einsum('bhqk,bhkd->bhqd', p[..., n_keys:], v_own)

    out = lax.map(chunk, (q_chunks, jnp.arange(n_chunks)))
    return out.transpose(1, 2, 0, 3, 4).reshape(b, h, s, dh)


def multiscale_pool(u, pool_w, pool_scale):
    b, s, _ = u.shape
    u32 = u.astype(jnp.float32)
    csum = jnp.pad(jnp.cumsum(u32, axis=1), ((0, 0), (1, 0), (0, 0)))
    count = jnp.arange(1, s + 1, dtype=jnp.float32)
    diffs = []
    for g, w in enumerate(POOL_WINDOWS):
        sl = slice(g * POOL_GROUP_DIM, (g + 1) * POOL_GROUP_DIM)
        cg = csum[..., sl]
        lo = jnp.pad(cg[:, :s + 1 - w], ((0, 0), (w - 1, 0), (0, 0)))
        mean = (cg[:, 1:] - lo) / jnp.minimum(count, float(w))[None, :, None]
        diffs.append(mean - u32[..., sl])
    d = jnp.stack(diffs, axis=2)
    y = jnp.einsum('bsgc,gce->bsge', d, pool_w.astype(jnp.float32)).reshape(b, s, POOL_WIDTH)
    return (y * pool_scale.astype(jnp.float32)).astype(u.dtype)


def hgrn2(q, f_logit, i, gate, lower_bound, out_norm):
    b, s, _ = q.shape
    n_c = s // HGRN_CHUNK
    z = f_logit.astype(jnp.float32)
    lb = lower_bound.astype(jnp.float32)
    log_f = jnp.logaddexp(jnp.log(jnp.maximum(lb, LB_FLOOR)), jnp.log1p(-lb) + jax.nn.log_sigmoid(z))
    key = (1.0 - lb) * jax.nn.sigmoid(-z)
    query = jax.nn.silu(q.astype(jnp.float32)) * HGRN_HEAD_DIM ** -0.5
    value = i.astype(jnp.float32)

    def chunks(t):
        return t.reshape(b, n_c, HGRN_CHUNK, HGRN_HEADS, HGRN_HEAD_DIM).transpose(1, 0, 3, 2, 4)

    causal = jnp.tril(jnp.ones((HGRN_CHUNK, HGRN_CHUNK), dtype=bool))[:, :, None]

    def step(state, inp):
        q_c, k_c, v_c, g_c = inp
        cum = jnp.cumsum(g_c, axis=2)
        last = cum[:, :, -1]
        o_inter = jnp.einsum('bhtk,bhkv->bhtv', q_c * jnp.exp(cum), state)
        diff = cum[:, :, :, None, :] - cum[:, :, None, :, :]
        decay = jnp.where(causal, jnp.exp(jnp.minimum(diff, 0.0)), 0.0)
        scores = jnp.einsum('bhtk,bhsk,bhtsk->bhts', q_c, k_c, decay)
        o_intra = jnp.einsum('bhts,bhsv->bhtv', scores, v_c)
        k_dec = k_c * jnp.exp(last[:, :, None, :] - cum)
        state = jnp.exp(last)[..., None] * state + jnp.einsum('bhsk,bhsv->bhkv', k_dec, v_c)
        return state, o_inter + o_intra

    state0 = jnp.zeros((b, HGRN_HEADS, HGRN_HEAD_DIM, HGRN_HEAD_DIM), jnp.float32)
    _, o = lax.scan(step, state0, (chunks(query), chunks(key), chunks(value), chunks(log_f)))
    o = o.transpose(1, 0, 3, 2, 4).reshape(b, s, HGRN_HEADS, HGRN_HEAD_DIM)
    g = gate.astype(jnp.float32).reshape(b, s, HGRN_HEADS, HGRN_HEAD_DIM)
    y = rms_norm(o, out_norm) * jax.nn.silu(g)
    return y.reshape(b, s, HGRN_WIDTH).astype(q.dtype)


def token_mixing(h, w_in, q_norm, k_norm, pool_w, pool_scale, lower_bound, hgrn_out_norm, w_out, pos):
    b, s, _ = h.shape
    proj = h @ w_in
    offsets = []
    acc = 0
    for sz in SPLIT_SIZES[:-1]:
        acc += sz
        offsets.append(acc)
    q_a, k_a, v_a, u_p, q_h, f_h, i_h, g_h = jnp.split(proj, offsets, axis=-1)

    def heads(t):
        return t.reshape(b, s, ATTN_HEADS, HEAD_DIM).transpose(0, 2, 1, 3)

    q = apply_rope(rms_norm(heads(q_a), q_norm), pos)
    k = apply_rope(rms_norm(heads(k_a), k_norm), pos)
    y_attn = moba_attention(q, k, heads(v_a)).transpose(0, 2, 1, 3).reshape(b, s, ATTN_WIDTH)
    y_pool = multiscale_pool(u_p, pool_w, pool_scale)
    y_hgrn = hgrn2(q_h, f_h, i_h, g_h, lower_bound, hgrn_out_norm)
    y = jnp.concatenate([y_attn, y_pool, y_hgrn], axis=-1)
    return y @ w_out


def setup_inputs(seed: int = 0) -> dict:
    key = jax.random.key(seed)
    ks = jax.random.split(key, 18)
    f32 = jnp.float32

    def normal(k, shape, scale):
        return jax.random.normal(k, shape, f32) * scale

    def gain(k, shape):
        return 1.0 + 0.05 * jax.random.normal(k, shape, f32)

    return {
        'x': normal(ks[0], (BATCH, SEQ, D_MODEL), 1.0),
        'ffn1_norm': gain(ks[1], (DEPTH, D_MODEL)),
        'ffn1_w_gate': normal(ks[2], (DEPTH, D_MODEL, D_FF), D_MODEL ** -0.5),
        'ffn1_w_up': normal(ks[3], (DEPTH, D_MODEL, D_FF), D_MODEL ** -0.5),
        'ffn1_w_down': normal(ks[4], (DEPTH, D_FF, D_MODEL), D_FF ** -0.5),
        'mix_norm': gain(ks[5], (DEPTH, D_MODEL)),
        'w_in': normal(ks[6], (DEPTH, D_MODEL, IN_WIDTH), D_MODEL ** -0.5),
        'q_norm': gain(ks[7], (DEPTH, HEAD_DIM)),
        'k_norm': gain(ks[8], (DEPTH, HEAD_DIM)),
        'pool_w': normal(ks[9], (DEPTH, POOL_GROUPS, POOL_GROUP_DIM, POOL_GROUP_DIM), POOL_GROUP_DIM ** -0.5),
        'pool_scale': gain(ks[10], (DEPTH, POOL_WIDTH)),
        'hgrn_lb': normal(ks[11], (DEPTH, HGRN_WIDTH), 0.5),
        'hgrn_out_norm': gain(ks[12], (DEPTH, HGRN_HEAD_DIM)),
        'w_out': normal(ks[13], (DEPTH, MIX_WIDTH, D_MODEL), MIX_WIDTH ** -0.5),
        'ffn2_norm': gain(ks[14], (DEPTH, D_MODEL)),
        'ffn2_w_gate': normal(ks[15], (DEPTH, D_MODEL, D_FF), D_MODEL ** -0.5),
        'ffn2_w_up': normal(ks[16], (DEPTH, D_MODEL, D_FF), D_MODEL ** -0.5),
        'ffn2_w_down': normal(ks[17], (DEPTH, D_FF, D_MODEL), D_FF ** -0.5),
    }


def reference(x, ffn1_norm, ffn1_w_gate, ffn1_w_up, ffn1_w_down, mix_norm, w_in, q_norm, k_norm,
              pool_w, pool_scale, hgrn_lb, hgrn_out_norm, w_out, ffn2_norm, ffn2_w_gate, ffn2_w_up,
              ffn2_w_down):
    pos = jnp.arange(x.shape[1])
    lb_soft = jax.nn.softmax(hgrn_lb.astype(jnp.float32), axis=0)
    lower_bounds = jnp.concatenate([jnp.zeros_like(lb_soft[:1]), jnp.cumsum(lb_soft[:-1], axis=0)], axis=0)
    for l in range(DEPTH):
        x = x + 0.5 * swiglu(rms_norm(x, ffn1_norm[l]), ffn1_w_gate[l], ffn1_w_up[l], ffn1_w_down[l])
        x = x + token_mixing(rms_norm(x, mix_norm[l]), w_in[l], q_norm[l], k_norm[l], pool_w[l],
                             pool_scale[l], lower_bounds[l], hgrn_out_norm[l], w_out[l], pos)
        x = x + 0.5 * swiglu(rms_norm(x, ffn2_norm[l]), ffn2_w_gate[l], ffn2_w_up[l], ffn2_w_down[l])
    return x
```

```python
import functools

import jax
import jax.numpy as jnp
from jax import lax
from jax.experimental import pallas as pl
from jax.experimental.pallas import tpu as pltpu

F32 = jnp.float32
BF16 = jnp.bfloat16
HIGHEST = lax.Precision.HIGHEST

D_MODEL = 1024
DEPTH = 4
ATTN_HEADS = 8
HEAD_DIM = 64
HALF = HEAD_DIM // 2
ATTN_WIDTH = ATTN_HEADS * HEAD_DIM
HEAD_PAIRS = ATTN_HEADS // 2
PAIR_DIM = 2 * HEAD_DIM
POOL_WINDOWS = (2, 4, 8, 16)
POOL_GROUP_DIM = 64
POOL_WIDTH = 256
HGRN_HEAD_DIM = 64
HGRN_WIDTH = 256
REST_WIDTH = POOL_WIDTH + 4 * HGRN_WIDTH
D_FF = 2816
FF_CHUNK = 256
N_FF_CHUNKS = D_FF // FF_CHUNK
MOBA_BLOCK = 256
MOBA_TOPK = 3
ROPE_THETA = 10000.0
EPS = 1e-6
NEG_INF = -1e30
LB_FLOOR = 1e-20

FFN_ROWS = 512
PROJ_ROWS = 512
MIX_ROWS = 128
SUB = 16
VMEM_LIMIT = 56 * 1024 * 1024


def _rms_rows(x, gain_row):
    ms = jnp.mean(x * x, axis=-1, keepdims=True)
    return x * lax.rsqrt(ms + EPS) * gain_row


def _silu(x):
    return x * jax.nn.sigmoid(x)


def _swiglu_half_step(x, gain_row, wgu_ref, wd_ref):
    h = _rms_rows(x, gain_row).astype(BF16)
    acc = jnp.zeros_like(x)
    for c in range(N_FF_CHUNKS):
        gu = jnp.dot(h, wgu_ref[:, c * 2 * FF_CHUNK:(c + 1) * 2 * FF_CHUNK],
                     preferred_element_type=F32)
        a = (_silu(gu[:, :FF_CHUNK]) * gu[:, FF_CHUNK:]).astype(BF16)
        acc = acc + jnp.dot(a, wd_ref[c * FF_CHUNK:(c + 1) * FF_CHUNK, :],
                            preferred_element_type=F32)
    return x + 0.5 * acc


def _ffn_body(x_ref, gain_ref, wgu_ref, wd_ref, o_ref):
    o_ref[...] = _swiglu_half_step(x_ref[...], gain_ref[...], wgu_ref, wd_ref)


def _outproj_ffn_body(x_ref, ya_ref, yr_ref, wo_ref, gain_ref, wgu_ref, wd_ref, o_ref):
    x = x_ref[...]
    x = x + jnp.dot(ya_ref[...].astype(BF16), wo_ref[:ATTN_WIDTH, :], preferred_element_type=F32)
    x = x + jnp.dot(yr_ref[...].astype(BF16), wo_ref[ATTN_WIDTH:, :], preferred_element_type=F32)
    o_ref[...] = _swiglu_half_step(x, gain_ref[...], wgu_ref, wd_ref)


def _const_spec(shape):
    return pl.BlockSpec(shape, lambda *_: (0,) * len(shape), pipeline_mode=pl.Buffered(1))


def _ffn(x2d, gain_row, wgu, wd):
    n = x2d.shape[0]
    row_spec = pl.BlockSpec((FFN_ROWS, D_MODEL), lambda i: (i, 0))
    return pl.pallas_call(
        _ffn_body,
        out_shape=jax.ShapeDtypeStruct((n, D_MODEL), F32),
        grid=(n // FFN_ROWS,),
        in_specs=[row_spec, _const_spec((1, D_MODEL)), _const_spec((D_MODEL, 2 * D_FF)),
                  _const_spec((D_FF, D_MODEL))],
        out_specs=row_spec,
        compiler_params=pltpu.CompilerParams(dimension_semantics=("arbitrary",),
                                             vmem_limit_bytes=VMEM_LIMIT),
        name="ffn",
    )(x2d, gain_row, wgu, wd)


def _outproj_ffn(x2d, ya, yr, wo, gain_row, wgu, wd):
    n = x2d.shape[0]
    row_spec = pl.BlockSpec((FFN_ROWS, D_MODEL), lambda i: (i, 0))
    half_spec = pl.BlockSpec((FFN_ROWS, ATTN_WIDTH), lambda i: (i, 0))
    return pl.pallas_call(
        _outproj_ffn_body,
        out_shape=jax.ShapeDtypeStruct((n, D_MODEL), F32),
        grid=(n // FFN_ROWS,),
        in_specs=[row_spec, half_spec, half_spec, _const_spec((D_MODEL, D_MODEL)),
                  _const_spec((1, D_MODEL)), _const_spec((D_MODEL, 2 * D_FF)),
                  _const_spec((D_FF, D_MODEL))],
        out_specs=row_spec,
        compiler_params=pltpu.CompilerParams(dimension_semantics=("arbitrary",),
                                             vmem_limit_bytes=VMEM_LIMIT),
        name="outproj_ffn",
    )(x2d, ya, yr, wo, gain_row, wgu, wd)


def _proj_body(x_ref, gain_ref, wt_ref, wr_ref, qn_ref, kn_ref, cos_ref, sin_ref,
               qt_ref, k_ref, vt_ref, rest_ref):
    h = _rms_rows(x_ref[0], gain_ref[...]).astype(BF16)
    rest_ref[0] = jnp.dot(h, wr_ref[...], preferred_element_type=F32)
    qkvt = lax.dot_general(wt_ref[...], h, (((1,), (1,)), ((), ())), preferred_element_type=F32)
    cos = cos_ref[...]
    sin = sin_ref[...]

    def norm_rope(t, gain_col):
        ms = jnp.mean(t * t, axis=0, keepdims=True)
        tn = t * lax.rsqrt(ms + EPS) * gain_col
        x1, x2 = tn[:HALF], tn[HALF:]
        return jnp.concatenate([x1 * cos - x2 * sin, x2 * cos + x1 * sin], axis=0)

    n_blk = PROJ_ROWS // MOBA_BLOCK
    for p in range(HEAD_PAIRS):
        lo = p * PAIR_DIM
        q_pair = jnp.concatenate(
            [norm_rope(qkvt[lo + e * HEAD_DIM:lo + (e + 1) * HEAD_DIM], qn_ref[...]) for e in range(2)],
            axis=0)
        k_pair = jnp.concatenate(
            [norm_rope(qkvt[ATTN_WIDTH + lo + e * HEAD_DIM:ATTN_WIDTH + lo + (e + 1) * HEAD_DIM],
                       kn_ref[...]) for e in range(2)], axis=0)
        v_pair = qkvt[2 * ATTN_WIDTH + lo:2 * ATTN_WIDTH + lo + PAIR_DIM].astype(BF16)
        k_ref[0, :, lo:lo + PAIR_DIM] = k_pair.T.astype(BF16)
        for b in range(n_blk):
            qt_ref[0, p, b] = q_pair[:, b * MOBA_BLOCK:(b + 1) * MOBA_BLOCK]
            vt_ref[0, p, b] = v_pair[:, b * MOBA_BLOCK:(b + 1) * MOBA_BLOCK]


def _proj(x3d, gain_row, wt, wr, qn_col, kn_col, cos_t, sin_t):
    bsz, seq, _ = x3d.shape
    nb = seq // MOBA_BLOCK
    n_blk = PROJ_ROWS // MOBA_BLOCK
    blk_spec = pl.BlockSpec((1, HEAD_PAIRS, n_blk, PAIR_DIM, MOBA_BLOCK), lambda b, i: (b, 0, i, 0, 0))
    return pl.pallas_call(
        _proj_body,
        out_shape=(jax.ShapeDtypeStruct((bsz, HEAD_PAIRS, nb, PAIR_DIM, MOBA_BLOCK), F32),
                   jax.ShapeDtypeStruct((bsz, seq, ATTN_WIDTH), BF16),
                   jax.ShapeDtypeStruct((bsz, HEAD_PAIRS, nb, PAIR_DIM, MOBA_BLOCK), BF16),
                   jax.ShapeDtypeStruct((bsz, seq, REST_WIDTH), F32)),
        grid=(bsz, seq // PROJ_ROWS),
        in_specs=[pl.BlockSpec((1, PROJ_ROWS, D_MODEL), lambda b, i: (b, i, 0)),
                  _const_spec((1, D_MODEL)),
                  _const_spec((3 * ATTN_WIDTH, D_MODEL)),
                  _const_spec((D_MODEL, REST_WIDTH)),
                  _const_spec((HEAD_DIM, 1)), _const_spec((HEAD_DIM, 1)),
                  pl.BlockSpec((HALF, PROJ_ROWS), lambda b, i: (0, i)),
                  pl.BlockSpec((HALF, PROJ_ROWS), lambda b, i: (0, i))],
        out_specs=(blk_spec,
                   pl.BlockSpec((1, PROJ_ROWS, ATTN_WIDTH), lambda b, i: (b, i, 0)),
                   blk_spec,
                   pl.BlockSpec((1, PROJ_ROWS, REST_WIDTH), lambda b, i: (b, i, 0))),
        compiler_params=pltpu.CompilerParams(dimension_semantics=("arbitrary", "arbitrary"),
                                             vmem_limit_bytes=VMEM_LIMIT),
        name="proj",
    )(x3d, gain_row, wt, wr, qn_col, kn_col, cos_t, sin_t)


def _attn_body(qt_ref, k_ref, vt_ref, y_ref, kmean_scr, bias_scr, *, nb):
    i = pl.program_id(2)

    @pl.when(i == 0)
    def _():
        for j in range(nb):
            kb = k_ref[0, j * MOBA_BLOCK:(j + 1) * MOBA_BLOCK, :].astype(F32)
            kmean_scr[j:j + 1, :] = jnp.sum(kb, axis=0, keepdims=True) * (1.0 / MOBA_BLOCK)

    qt = qt_ref[0, 0, 0]
    kmean = kmean_scr[...]
    lane = lax.broadcasted_iota(jnp.int32, (nb, PAIR_DIM), 1)
    qrow = lax.broadcasted_iota(jnp.int32, (PAIR_DIM, MOBA_BLOCK), 0)
    blk = lax.broadcasted_iota(jnp.int32, (nb, MOBA_BLOCK), 0)
    key_pos = lax.broadcasted_iota(jnp.int32, (MOBA_BLOCK, MOBA_BLOCK), 0)
    q_pos = lax.broadcasted_iota(jnp.int32, (MOBA_BLOCK, MOBA_BLOCK), 1)
    causal_bias = jnp.where(key_pos <= q_pos, 0.0, NEG_INF)
    valid = blk < i

    q_heads = []
    for e in range(2):
        in_head = (lane >= e * HEAD_DIM) & (lane < (e + 1) * HEAD_DIM)
        gate = jnp.dot(jnp.where(in_head, kmean, 0.0), qt, precision=HIGHEST,
                       preferred_element_type=F32)
        sel = jnp.zeros((nb, MOBA_BLOCK), F32)
        for _ in range(MOBA_TOPK):
            cand = jnp.where(valid & (sel == 0.0), gate, -jnp.inf)
            best = jnp.max(cand, axis=0, keepdims=True)
            is_best = (cand == best) & (best > -jnp.inf)
            first = jnp.min(jnp.where(is_best, blk, nb), axis=0, keepdims=True)
            sel = jnp.where(blk == first, 1.0, sel)
        bias_scr[e] = jnp.where(sel > 0.0, 0.0, NEG_INF)
        in_rows = (qrow >= e * HEAD_DIM) & (qrow < (e + 1) * HEAD_DIM)
        q_heads.append(jnp.where(in_rows, qt * (HEAD_DIM ** -0.5), 0.0).astype(BF16))

    def block_scores(j, e):
        kj = k_ref[0, pl.ds(pl.multiple_of(j * MOBA_BLOCK, MOBA_BLOCK), MOBA_BLOCK), :]
        return jnp.dot(kj, q_heads[e], preferred_element_type=F32)

    def values_t(j, e):
        return vt_ref[0, 0, j][e * HEAD_DIM:(e + 1) * HEAD_DIM, :]

    init = []
    for e in range(2):
        s = block_scores(i, e) + causal_bias
        m = jnp.max(s, axis=0, keepdims=True)
        p = jnp.exp(s - m)
        l = jnp.sum(p, axis=0, keepdims=True)
        acc = jnp.dot(values_t(i, e), p.astype(BF16), preferred_element_type=F32)
        init += [m, l, acc]

    def body(j, carry):
        out = []
        for e in range(2):
            m, l, acc = carry[3 * e:3 * e + 3]
            s = block_scores(j, e) + bias_scr[e, pl.ds(j, 1), :]
            m_new = jnp.maximum(m, jnp.max(s, axis=0, keepdims=True))
            alpha = jnp.exp(m - m_new)
            p = jnp.exp(s - m_new)
            l = alpha * l + jnp.sum(p, axis=0, keepdims=True)
            acc = alpha * acc + jnp.dot(values_t(j, e), p.astype(BF16), preferred_element_type=F32)
            out += [m_new, l, acc]
        return tuple(out)

    fin = lax.fori_loop(0, i, body, tuple(init))
    o_pair = jnp.concatenate([fin[2] / fin[1], fin[5] / fin[4]], axis=0)
    y_ref[0] = o_pair.T


def _attn(qt, k, vt):
    bsz, _, nb, _, _ = qt.shape
    seq = nb * MOBA_BLOCK
    return pl.pallas_call(
        functools.partial(_attn_body, nb=nb),
        out_shape=jax.ShapeDtypeStruct((bsz, seq, ATTN_WIDTH), F32),
        grid=(bsz, HEAD_PAIRS, nb),
        in_specs=[pl.BlockSpec((1, 1, 1, PAIR_DIM, MOBA_BLOCK), lambda b, p, i: (b, p, i, 0, 0)),
                  pl.BlockSpec((1, seq, PAIR_DIM), lambda b, p, i: (b, 0, p)),
                  pl.BlockSpec((1, 1, nb, PAIR_DIM, MOBA_BLOCK), lambda b, p, i: (b, p, 0, 0, 0))],
        out_specs=pl.BlockSpec((1, MOBA_BLOCK, PAIR_DIM), lambda b, p, i: (b, i, p)),
        scratch_shapes=[pltpu.VMEM((nb, PAIR_DIM), F32),
                        pltpu.VMEM((2, nb, MOBA_BLOCK), F32)],
        compiler_params=pltpu.CompilerParams(
            dimension_semantics=("arbitrary", "arbitrary", "arbitrary"),
            vmem_limit_bytes=VMEM_LIMIT),
        name="moba_attn",
    )(qt, k, vt)


def _mix_body(r_ref, lb_ref, onorm_ref, poolw_ref, pscale_ref, y_ref, st_scr, uprev_scr):
    t = pl.program_id(1)

    @pl.when(t == 0)
    def _():
        st_scr[...] = jnp.zeros_like(st_scr)
        uprev_scr[...] = jnp.zeros_like(uprev_scr)

    rows, w = MIX_ROWS, HGRN_WIDTH
    u = r_ref[0, :, 0:POOL_WIDTH]
    qh = r_ref[0, :, POOL_WIDTH:POOL_WIDTH + w]
    z = r_ref[0, :, POOL_WIDTH + w:POOL_WIDTH + 2 * w]
    value = r_ref[0, :, POOL_WIDTH + 2 * w:POOL_WIDTH + 3 * w]
    og = r_ref[0, :, POOL_WIDTH + 3 * w:POOL_WIDTH + 4 * w]

    lane = lax.broadcasted_iota(jnp.int32, (rows, w), 1)
    row = lax.broadcasted_iota(jnp.int32, (rows, w), 0)

    uext = jnp.concatenate([uprev_scr[...], u], axis=0)
    sums = []
    acc = uext
    for shift in (1, 2, 4, 8):
        acc = acc + pltpu.roll(acc, shift, 0)
        sums.append(acc[rows:])
    group = lane // POOL_GROUP_DIM
    win_sum = jnp.where(group == 0, sums[0],
                        jnp.where(group == 1, sums[1], jnp.where(group == 2, sums[2], sums[3])))
    window = jnp.where(group == 0, POOL_WINDOWS[0],
                       jnp.where(group == 1, POOL_WINDOWS[1],
                                 jnp.where(group == 2, POOL_WINDOWS[2], POOL_WINDOWS[3])))
    count = jnp.minimum(t * rows + row + 1, window).astype(F32)
    diff = win_sum / count - u
    y_pool = jnp.dot(diff, poolw_ref[...], precision=HIGHEST,
                     preferred_element_type=F32) * pscale_ref[...]
    uprev_scr[...] = u

    lb = lb_ref[...]
    log_sig = jnp.minimum(z, 0.0) - jnp.log1p(jnp.exp(-jnp.abs(z)))
    a_term = jnp.log(jnp.maximum(lb, LB_FLOOR))
    b_term = jnp.log1p(-lb) + log_sig
    log_f = jnp.maximum(a_term, b_term) + jnp.log1p(jnp.exp(-jnp.abs(a_term - b_term)))
    key = (1.0 - lb) * jax.nn.sigmoid(-z)
    query = _silu(qh) * (HGRN_HEAD_DIM ** -0.5)

    r_t = lax.broadcasted_iota(jnp.int32, (rows, rows), 0)
    c_t = lax.broadcasted_iota(jnp.int32, (rows, rows), 1)
    same_sub = (r_t // SUB) == (c_t // SUB)
    tri = jnp.where(same_sub & (c_t <= r_t), 1.0, 0.0)
    ones_sub = jnp.where(same_sub, 1.0, 0.0)
    cum = jnp.dot(tri, log_f, precision=HIGHEST, preferred_element_type=F32)
    tot = jnp.dot(ones_sub, log_f, precision=HIGHEST, preferred_element_type=F32)
    k_dec = key * jnp.exp(tot - cum)
    q_dec = query * jnp.exp(cum)
    e_tot = jnp.exp(tot)
    value_t = value.T.astype(BF16)

    r_w = lax.broadcasted_iota(jnp.int32, (w, w), 0)
    c_w = lax.broadcasted_iota(jnp.int32, (w, w), 1)
    same_head = (r_w // HGRN_HEAD_DIM) == (c_w // HGRN_HEAD_DIM)
    head_ones = jnp.where(same_head, 1.0, 0.0).astype(BF16)
    head_mask = jnp.where(same_head, 1.0, 0.0)
    sub_row = lax.broadcasted_iota(jnp.int32, (SUB, w), 0)

    st = st_scr[...]
    outs = []
    for a in range(rows // SUB):
        rs = slice(a * SUB, (a + 1) * SUB)
        o_inter = lax.dot_general(q_dec[rs].astype(BF16), st.astype(BF16),
                                  (((1,), (1,)), ((), ())), preferred_element_type=F32)
        cum_a, q_a, k_a, v_a = cum[rs], query[rs], key[rs], value[rs]
        pieces = []
        for s in range(SUB):
            decay = jnp.exp(jnp.minimum(cum_a - cum_a[s:s + 1], 0.0))
            pieces.append(jnp.where(sub_row >= s, decay * (q_a * k_a[s:s + 1]), 0.0))
        e_all = jnp.concatenate(pieces, axis=0)
        e_hi = e_all.astype(BF16)
        e_lo = (e_all - e_hi.astype(F32)).astype(BF16)
        scores = (jnp.dot(e_hi, head_ones, preferred_element_type=F32)
                  + jnp.dot(e_lo, head_ones, preferred_element_type=F32))
        o_diag = jnp.zeros((SUB, w), F32)
        for s in range(SUB):
            o_diag = o_diag + scores[s * SUB:(s + 1) * SUB] * v_a[s:s + 1]
        outs.append(o_inter + o_diag)
        k_sub = jnp.where((row >= a * SUB) & (row < (a + 1) * SUB), k_dec, 0.0).astype(BF16)
        upd_t = jnp.dot(value_t, k_sub, preferred_element_type=F32)
        st = e_tot[a * SUB:a * SUB + 1] * st + upd_t * head_mask
    st_scr[...] = st

    o = jnp.concatenate(outs, axis=0)
    ms = jnp.dot(o * o, head_mask, precision=HIGHEST,
                 preferred_element_type=F32) * (1.0 / HGRN_HEAD_DIM)
    y_hgrn = o * lax.rsqrt(ms + EPS) * onorm_ref[...] * _silu(og)
    y_ref[0, :, 0:POOL_WIDTH] = y_pool
    y_ref[0, :, POOL_WIDTH:] = y_hgrn


def _mix(rest, lb_row, onorm_row, poolw_bd, pscale_row):
    bsz, seq, _ = rest.shape
    return pl.pallas_call(
        _mix_body,
        out_shape=jax.ShapeDtypeStruct((bsz, seq, POOL_WIDTH + HGRN_WIDTH), F32),
        grid=(bsz, seq // MIX_ROWS),
        in_specs=[pl.BlockSpec((1, MIX_ROWS, REST_WIDTH), lambda b, t: (b, t, 0)),
                  _const_spec((1, HGRN_WIDTH)), _const_spec((1, HGRN_WIDTH)),
                  _const_spec((POOL_WIDTH, POOL_WIDTH)), _const_spec((1, POOL_WIDTH))],
        out_specs=pl.BlockSpec((1, MIX_ROWS, POOL_WIDTH + HGRN_WIDTH), lambda b, t: (b, t, 0)),
        scratch_shapes=[pltpu.VMEM((HGRN_WIDTH, HGRN_WIDTH), F32),
                        pltpu.VMEM((MIX_ROWS, POOL_WIDTH), F32)],
        compiler_params=pltpu.CompilerParams(dimension_semantics=("arbitrary", "arbitrary"),
                                             vmem_limit_bytes=VMEM_LIMIT),
        name="pool_hgrn",
    )(rest, lb_row, onorm_row, poolw_bd, pscale_row)


def _interleave_gate_up(w_gate, w_up):
    d = w_gate.shape[0]
    g = w_gate.reshape(d, N_FF_CHUNKS, 1, FF_CHUNK)
    u = w_up.reshape(d, N_FF_CHUNKS, 1, FF_CHUNK)
    return jnp.concatenate([g, u], axis=2).reshape(d, 2 * D_FF).astype(BF16)


def _block_diag(blocks):
    g, c, e = blocks.shape
    out = jnp.zeros((g * c, g * e), blocks.dtype)
    for i in range(g):
        out = out.at[i * c:(i + 1) * c, i * e:(i + 1) * e].set(blocks[i])
    return out


def kernel(x, ffn1_norm, ffn1_w_gate, ffn1_w_up, ffn1_w_down, mix_norm, w_in, q_norm, k_norm,
           pool_w, pool_scale, hgrn_lb, hgrn_out_norm, w_out, ffn2_norm, ffn2_w_gate, ffn2_w_up,
           ffn2_w_down):
    bsz, seq, d = x.shape
    n = bsz * seq

    inv_freq = ROPE_THETA ** (-jnp.arange(HALF, dtype=F32) / HALF)
    ang = jnp.arange(seq).astype(F32)[:, None] * inv_freq[None, :]
    cos_t, sin_t = jnp.cos(ang).T, jnp.sin(ang).T

    lb_soft = jax.nn.softmax(hgrn_lb.astype(F32), axis=0)
    lower_bounds = jnp.concatenate([jnp.zeros_like(lb_soft[:1]), jnp.cumsum(lb_soft[:-1], axis=0)],
                                   axis=0)

    x2d = x.reshape(n, d)
    for l in range(DEPTH):
        x2d = _ffn(x2d, ffn1_norm[l][None, :], _interleave_gate_up(ffn1_w_gate[l], ffn1_w_up[l]),
                   ffn1_w_down[l].astype(BF16))
        wt = w_in[l][:, :3 * ATTN_WIDTH].T.astype(BF16)
        wr = w_in[l][:, 3 * ATTN_WIDTH:].astype(BF16)
        qt, k, vt, rest = _proj(x2d.reshape(bsz, seq, d), mix_norm[l][None, :], wt, wr,
                                q_norm[l][:, None], k_norm[l][:, None], cos_t, sin_t)
        y_attn = _attn(qt, k, vt)
        y_rest = _mix(rest, lower_bounds[l][None, :], jnp.tile(hgrn_out_norm[l], 4)[None, :],
                      _block_diag(pool_w[l].astype(F32)), pool_scale[l][None, :])
        x2d = _outproj_ffn(x2d, y_attn.reshape(n, ATTN_WIDTH), y_rest.reshape(n, ATTN_WIDTH),
                           w_out[l].astype(BF16), ffn2_norm[l][None, :],
                           _interleave_gate_up(ffn2_w_gate[l], ffn2_w_up[l]),
                           ffn2_w_down[l].astype(BF16))
    return x2d.reshape(bsz, seq, d)
```

```python
import functools

import jax
import jax.numpy as jnp
from jax import lax
from jax.experimental import pallas as pl
from jax.experimental.pallas import tpu as pltpu

F32 = jnp.float32
BF16 = jnp.bfloat16
HIGHEST = lax.Precision.HIGHEST

D_MODEL = 1024
DEPTH = 4
ATTN_HEADS = 8
HEAD_DIM = 64
HALF = HEAD_DIM // 2
ATTN_WIDTH = ATTN_HEADS * HEAD_DIM
HEAD_PAIRS = ATTN_HEADS // 2
PAIR_DIM = 2 * HEAD_DIM
POOL_WINDOWS = (2, 4, 8, 16)
POOL_GROUP_DIM = 64
POOL_WIDTH = 256
HGRN_HEAD_DIM = 64
HGRN_WIDTH = 256
REST_WIDTH = POOL_WIDTH + 4 * HGRN_WIDTH
D_FF = 2816
FF_CHUNK = 256
N_FF_CHUNKS = D_FF // FF_CHUNK
MOBA_BLOCK = 256
MOBA_TOPK = 3
ROPE_THETA = 10000.0
EPS = 1e-6
NEG_INF = -1e30
LB_FLOOR = 1e-20

FFN_ROWS = 512
PROJ_ROWS = 512
MIX_ROWS = 128
SUB = 16
ATTN_PAIRS = 2
ONES_ROWS = 16
MASK_BIAS = 2 * NEG_INF
LOG2_E = 1.4426950408889634
VMEM_LIMIT = 56 * 1024 * 1024


def _rms_rows(x, gain_row):
    ms = jnp.mean(x * x, axis=-1, keepdims=True)
    return x * lax.rsqrt(ms + EPS) * gain_row


def _silu(x):
    return x * jax.nn.sigmoid(x)


def _swiglu_half_step(x, gain_row, wg_ref, wu_ref, wd_ref):
    h = _rms_rows(x, gain_row).astype(BF16)
    acc = jnp.zeros_like(x)
    for c in range(N_FF_CHUNKS):
        cols = slice(c * FF_CHUNK, (c + 1) * FF_CHUNK)
        g = jnp.dot(h, wg_ref[0, :, cols], preferred_element_type=F32)
        u = jnp.dot(h, wu_ref[0, :, cols], preferred_element_type=F32)
        acc = acc + jnp.dot((_silu(g) * u).astype(BF16), wd_ref[0, cols, :],
                            preferred_element_type=F32)
    return x + 0.5 * acc


def _ffn_body(x_ref, gain_ref, wg_ref, wu_ref, wd_ref, o_ref):
    o_ref[...] = _swiglu_half_step(x_ref[...], gain_ref[0], wg_ref, wu_ref, wd_ref)


def _outproj_ffn_body(x_ref, ya_ref, yr_ref, wo_ref, gain_ref, wg_ref, wu_ref, wd_ref, o_ref):
    x = x_ref[...]
    x = x + jnp.dot(ya_ref[...].astype(BF16), wo_ref[0, :ATTN_WIDTH, :], preferred_element_type=F32)
    x = x + jnp.dot(yr_ref[...].astype(BF16), wo_ref[0, ATTN_WIDTH:, :], preferred_element_type=F32)
    o_ref[...] = _swiglu_half_step(x, gain_ref[0], wg_ref, wu_ref, wd_ref)


def _const_spec(shape):
    return pl.BlockSpec(shape, lambda *_: (0,) * len(shape), pipeline_mode=pl.Buffered(1))


def _layer_spec(shape, layer):
    return pl.BlockSpec((1,) + shape, lambda *_: (layer,) + (0,) * len(shape),
                        pipeline_mode=pl.Buffered(1))


def _ffn_weight_specs(layer):
    return [_layer_spec((1, D_MODEL), layer), _layer_spec((D_MODEL, D_FF), layer),
            _layer_spec((D_MODEL, D_FF), layer), _layer_spec((D_FF, D_MODEL), layer)]


def _ffn(x2d, layer, gains, wg, wu, wd):
    n = x2d.shape[0]
    row_spec = pl.BlockSpec((FFN_ROWS, D_MODEL), lambda i: (i, 0))
    return pl.pallas_call(
        _ffn_body,
        out_shape=jax.ShapeDtypeStruct((n, D_MODEL), F32),
        grid=(n // FFN_ROWS,),
        in_specs=[row_spec] + _ffn_weight_specs(layer),
        out_specs=row_spec,
        compiler_params=pltpu.CompilerParams(dimension_semantics=("arbitrary",),
                                             vmem_limit_bytes=VMEM_LIMIT),
        name="ffn",
    )(x2d, gains, wg, wu, wd)


def _outproj_ffn(x2d, ya, yr, layer, wo, gains, wg, wu, wd):
    n = x2d.shape[0]
    row_spec = pl.BlockSpec((FFN_ROWS, D_MODEL), lambda i: (i, 0))
    half_spec = pl.BlockSpec((FFN_ROWS, ATTN_WIDTH), lambda i: (i, 0))
    return pl.pallas_call(
        _outproj_ffn_body,
        out_shape=jax.ShapeDtypeStruct((n, D_MODEL), F32),
        grid=(n // FFN_ROWS,),
        in_specs=[row_spec, half_spec, half_spec, _layer_spec((D_MODEL, D_MODEL), layer)]
        + _ffn_weight_specs(layer),
        out_specs=row_spec,
        compiler_params=pltpu.CompilerParams(dimension_semantics=("arbitrary",),
                                             vmem_limit_bytes=VMEM_LIMIT),
        name="outproj_ffn",
    )(x2d, ya, yr, wo, gains, wg, wu, wd)


def _proj_body(x_ref, gain_ref, wt_ref, wr_ref, qn_ref, kn_ref, cos_ref, sin_ref,
               qt_ref, k_ref, vt_ref, rest_ref):
    h = _rms_rows(x_ref[0], gain_ref[...]).astype(BF16)
    rest_ref[0] = jnp.dot(h, wr_ref[...], preferred_element_type=F32)
    qkvt = lax.dot_general(wt_ref[...], h, (((1,), (1,)), ((), ())), preferred_element_type=F32)
    cos = cos_ref[...]
    sin = sin_ref[...]

    def norm_rope(t, gain_col):
        ms = jnp.mean(t * t, axis=0, keepdims=True)
        tn = t * lax.rsqrt(ms + EPS) * gain_col
        x1, x2 = tn[:HALF], tn[HALF:]
        return jnp.concatenate([x1 * cos - x2 * sin, x2 * cos + x1 * sin], axis=0)

    n_blk = PROJ_ROWS // MOBA_BLOCK
    for p in range(HEAD_PAIRS):
        lo = p * PAIR_DIM
        q_pair = jnp.concatenate(
            [norm_rope(qkvt[lo + e * HEAD_DIM:lo + (e + 1) * HEAD_DIM], qn_ref[...]) for e in range(2)],
            axis=0)
        k_pair = jnp.concatenate(
            [norm_rope(qkvt[ATTN_WIDTH + lo + e * HEAD_DIM:ATTN_WIDTH + lo + (e + 1) * HEAD_DIM],
                       kn_ref[...]) for e in range(2)], axis=0)
        v_pair = qkvt[2 * ATTN_WIDTH + lo:2 * ATTN_WIDTH + lo + PAIR_DIM].astype(BF16)
        k_ref[0, :, lo:lo + PAIR_DIM] = k_pair.T.astype(BF16)
        for b in range(n_blk):
            qt_ref[0, p, b] = q_pair[:, b * MOBA_BLOCK:(b + 1) * MOBA_BLOCK]
            vt_ref[0, p, b] = v_pair[:, b * MOBA_BLOCK:(b + 1) * MOBA_BLOCK]


def _proj(x3d, gain_row, wt, wr, qn_col, kn_col, cos_t, sin_t):
    bsz, seq, _ = x3d.shape
    nb = seq // MOBA_BLOCK
    n_blk = PROJ_ROWS // MOBA_BLOCK
    blk_spec = pl.BlockSpec((1, HEAD_PAIRS, n_blk, PAIR_DIM, MOBA_BLOCK), lambda b, i: (b, 0, i, 0, 0))
    return pl.pallas_call(
        _proj_body,
        out_shape=(jax.ShapeDtypeStruct((bsz, HEAD_PAIRS, nb, PAIR_DIM, MOBA_BLOCK), F32),
                   jax.ShapeDtypeStruct((bsz, seq, ATTN_WIDTH), BF16),
                   jax.ShapeDtypeStruct((bsz, HEAD_PAIRS, nb, PAIR_DIM, MOBA_BLOCK), BF16),
                   jax.ShapeDtypeStruct((bsz, seq, REST_WIDTH), F32)),
        grid=(bsz, seq // PROJ_ROWS),
        in_specs=[pl.BlockSpec((1, PROJ_ROWS, D_MODEL), lambda b, i: (b, i, 0)),
                  _const_spec((1, D_MODEL)),
                  _const_spec((3 * ATTN_WIDTH, D_MODEL)),
                  _const_spec((D_MODEL, REST_WIDTH)),
                  _const_spec((HEAD_DIM, 1)), _const_spec((HEAD_DIM, 1)),
                  pl.BlockSpec((HALF, PROJ_ROWS), lambda b, i: (0, i)),
                  pl.BlockSpec((HALF, PROJ_ROWS), lambda b, i: (0, i))],
        out_specs=(blk_spec,
                   pl.BlockSpec((1, PROJ_ROWS, ATTN_WIDTH), lambda b, i: (b, i, 0)),
                   blk_spec,
                   pl.BlockSpec((1, PROJ_ROWS, REST_WIDTH), lambda b, i: (b, i, 0))),
        compiler_params=pltpu.CompilerParams(dimension_semantics=("arbitrary", "arbitrary"),
                                             vmem_limit_bytes=VMEM_LIMIT),
        name="proj",
    )(x3d, gain_row, wt, wr, qn_col, kn_col, cos_t, sin_t)


def _attn_body(qt_ref, k_ref, vt_ref, y_ref, kmean_scr, bias_scr, *s_scrs, nb):
    i = pl.program_id(2)
    width = ATTN_PAIRS * PAIR_DIM

    @pl.when(i == 0)
    def _():
        for j in range(nb):
            kb = k_ref[0, j * MOBA_BLOCK:(j + 1) * MOBA_BLOCK, :].astype(F32)
            kmean_scr[j:j + 1, :] = jnp.sum(kb, axis=0, keepdims=True) * (1.0 / MOBA_BLOCK)

    kmean = kmean_scr[...]
    lane = lax.broadcasted_iota(jnp.int32, (nb, width), 1)
    qrow = lax.broadcasted_iota(jnp.int32, (PAIR_DIM, MOBA_BLOCK), 0)
    blk = lax.broadcasted_iota(jnp.int32, (nb, MOBA_BLOCK), 0)
    valid = blk < i

    heads = [(p, e) for p in range(ATTN_PAIRS) for e in range(2)]
    q_heads = []
    for h, (p, e) in enumerate(heads):
        qt = qt_ref[0, p, 0]
        lo = p * PAIR_DIM + e * HEAD_DIM
        km = jnp.where((lane >= lo) & (lane < lo + HEAD_DIM), kmean, 0.0)
        gate = jnp.dot(km[:, p * PAIR_DIM:(p + 1) * PAIR_DIM], qt, precision=HIGHEST,
                       preferred_element_type=F32)
        sel = jnp.zeros((nb, MOBA_BLOCK), F32)
        for _ in range(MOBA_TOPK):
            cand = jnp.where(valid & (sel == 0.0), gate, -jnp.inf)
            best = jnp.max(cand, axis=0, keepdims=True)
            is_best = (cand == best) & (best > -jnp.inf)
            first = jnp.min(jnp.where(is_best, blk, nb), axis=0, keepdims=True)
            sel = jnp.where(blk == first, 1.0, sel)
        bias_scr[h] = jnp.where(sel > 0.0, 0.0, MASK_BIAS)
        in_rows = (qrow >= e * HEAD_DIM) & (qrow < (e + 1) * HEAD_DIM)
        q_heads.append(jnp.where(in_rows, qt * (HEAD_DIM ** -0.5 * LOG2_E), 0.0).astype(BF16))

    ones_rows = jnp.ones((ONES_ROWS, MOBA_BLOCK), BF16)

    def values_ext(j, h):
        p, e = heads[h]
        return jnp.concatenate([vt_ref[0, p, j][e * HEAD_DIM:(e + 1) * HEAD_DIM, :], ones_rows], axis=0)

    def scores(j, h):
        p = heads[h][0]
        kj = k_ref[0, pl.ds(pl.multiple_of(j * MOBA_BLOCK, MOBA_BLOCK), MOBA_BLOCK),
                   p * PAIR_DIM:(p + 1) * PAIR_DIM]
        return jnp.dot(kj, q_heads[h], preferred_element_type=F32)

    init = []
    for h in range(len(heads)):
        s = scores(0, h)
        s_scrs[h][0] = s
        init += [jnp.full((1, MOBA_BLOCK), NEG_INF, F32), jnp.max(s, axis=0, keepdims=True),
                 jnp.zeros((HEAD_DIM + ONES_ROWS, MOBA_BLOCK), F32)]

    def body(j, carry):
        slot = j & 1
        out = []
        for h in range(len(heads)):
            m, cmax, acc = carry[3 * h:3 * h + 3]
            brow = bias_scr[h, pl.ds(j, 1), :]
            m_new = jnp.maximum(m, cmax + brow)
            p = jnp.exp2(s_scrs[h][slot] + (brow - m_new)).astype(BF16)
            acc = jnp.exp2(m - m_new) * acc + jnp.dot(values_ext(j, h), p, preferred_element_type=F32)
            s_next = scores(j + 1, h)
            s_scrs[h][1 - slot] = s_next
            out += [m_new, jnp.max(s_next, axis=0, keepdims=True), acc]
        return tuple(out)

    fin = lax.fori_loop(0, i, body, tuple(init))

    key_pos = lax.broadcasted_iota(jnp.int32, (MOBA_BLOCK, MOBA_BLOCK), 0)
    q_pos = lax.broadcasted_iota(jnp.int32, (MOBA_BLOCK, MOBA_BLOCK), 1)
    causal_bias = jnp.where(key_pos <= q_pos, 0.0, MASK_BIAS)
    outs = []
    for h in range(len(heads)):
        m, _, acc = fin[3 * h:3 * h + 3]
        s = s_scrs[h][i & 1] + causal_bias
        m_new = jnp.maximum(m, jnp.max(s, axis=0, keepdims=True))
        p = jnp.exp2(s - m_new).astype(BF16)
        acc = jnp.exp2(m - m_new) * acc + jnp.dot(values_ext(i, h), p, preferred_element_type=F32)
        outs.append(acc[:HEAD_DIM] / acc[HEAD_DIM:HEAD_DIM + 1])
    for p in range(ATTN_PAIRS):
        o_pair = jnp.concatenate(outs[2 * p:2 * p + 2], axis=0)
        y_ref[0, :, p * PAIR_DIM:(p + 1) * PAIR_DIM] = o_pair.T


def _attn(qt, k, vt):
    bsz, _, nb, _, _ = qt.shape
    seq = nb * MOBA_BLOCK
    width = ATTN_PAIRS * PAIR_DIM
    n_heads = 2 * ATTN_PAIRS
    return pl.pallas_call(
        functools.partial(_attn_body, nb=nb),
        out_shape=jax.ShapeDtypeStruct((bsz, seq, ATTN_WIDTH), F32),
        grid=(bsz, HEAD_PAIRS // ATTN_PAIRS, nb),
        in_specs=[pl.BlockSpec((1, ATTN_PAIRS, 1, PAIR_DIM, MOBA_BLOCK), lambda b, p, i: (b, p, i, 0, 0)),
                  pl.BlockSpec((1, seq, width), lambda b, p, i: (b, 0, p)),
                  pl.BlockSpec((1, ATTN_PAIRS, nb, PAIR_DIM, MOBA_BLOCK),
                               lambda b, p, i: (b, p, 0, 0, 0))],
        out_specs=pl.BlockSpec((1, MOBA_BLOCK, width), lambda b, p, i: (b, i, p)),
        scratch_shapes=[pltpu.VMEM((nb, width), F32),
                        pltpu.VMEM((n_heads, nb, MOBA_BLOCK), F32)]
        + [pltpu.VMEM((2, MOBA_BLOCK, MOBA_BLOCK), F32) for _ in range(n_heads)],
        compiler_params=pltpu.CompilerParams(
            dimension_semantics=("arbitrary", "arbitrary", "arbitrary"),
            vmem_limit_bytes=VMEM_LIMIT),
        name="moba_attn",
    )(qt, k, vt)


def _mix_body(r_ref, lb_ref, onorm_ref, poolw_ref, pscale_ref, y_ref, st_scr, uprev_scr):
    t = pl.program_id(1)

    @pl.when(t == 0)
    def _():
        st_scr[...] = jnp.zeros_like(st_scr)
        uprev_scr[...] = jnp.zeros_like(uprev_scr)

    rows, w = MIX_ROWS, HGRN_WIDTH
    u = r_ref[0, :, 0:POOL_WIDTH]
    qh = r_ref[0, :, POOL_WIDTH:POOL_WIDTH + w]
    z = r_ref[0, :, POOL_WIDTH + w:POOL_WIDTH + 2 * w]
    value = r_ref[0, :, POOL_WIDTH + 2 * w:POOL_WIDTH + 3 * w]
    og = r_ref[0, :, POOL_WIDTH + 3 * w:POOL_WIDTH + 4 * w]

    lane = lax.broadcasted_iota(jnp.int32, (rows, w), 1)
    row = lax.broadcasted_iota(jnp.int32, (rows, w), 0)

    uext = jnp.concatenate([uprev_scr[...], u], axis=0)
    sums = []
    acc = uext
    for shift in (1, 2, 4, 8):
        acc = acc + pltpu.roll(acc, shift, 0)
        sums.append(acc[rows:])
    group = lane // POOL_GROUP_DIM
    win_sum = jnp.where(group == 0, sums[0],
                        jnp.where(group == 1, sums[1], jnp.where(group == 2, sums[2], sums[3])))
    window = jnp.where(group == 0, POOL_WINDOWS[0],
                       jnp.where(group == 1, POOL_WINDOWS[1],
                                 jnp.where(group == 2, POOL_WINDOWS[2], POOL_WINDOWS[3])))
    count = jnp.minimum(t * rows + row + 1, window).astype(F32)
    diff = win_sum / count - u
    y_pool = jnp.dot(diff, poolw_ref[...], precision=HIGHEST,
                     preferred_element_type=F32) * pscale_ref[...]
    uprev_scr[...] = u

    lb = lb_ref[...]
    log_sig = jnp.minimum(z, 0.0) - jnp.log1p(jnp.exp(-jnp.abs(z)))
    a_term = jnp.log(jnp.maximum(lb, LB_FLOOR))
    b_term = jnp.log1p(-lb) + log_sig
    log_f = jnp.maximum(a_term, b_term) + jnp.log1p(jnp.exp(-jnp.abs(a_term - b_term)))
    key = (1.0 - lb) * jax.nn.sigmoid(-z)
    query = _silu(qh) * (HGRN_HEAD_DIM ** -0.5)

    r_t = lax.broadcasted_iota(jnp.int32, (rows, rows), 0)
    c_t = lax.broadcasted_iota(jnp.int32, (rows, rows), 1)
    same_sub = (r_t // SUB) == (c_t // SUB)
    tri = jnp.where(same_sub & (c_t <= r_t), 1.0, 0.0)
    ones_sub = jnp.where(same_sub, 1.0, 0.0)
    cum = jnp.dot(tri, log_f, precision=HIGHEST, preferred_element_type=F32)
    tot = jnp.dot(ones_sub, log_f, precision=HIGHEST, preferred_element_type=F32)
    k_dec = key * jnp.exp(tot - cum)
    q_dec = query * jnp.exp(cum)
    e_tot = jnp.exp(tot)
    value_t = value.T.astype(BF16)

    r_w = lax.broadcasted_iota(jnp.int32, (w, w), 0)
    c_w = lax.broadcasted_iota(jnp.int32, (w, w), 1)
    same_head = (r_w // HGRN_HEAD_DIM) == (c_w // HGRN_HEAD_DIM)
    head_ones = jnp.where(same_head, 1.0, 0.0).astype(BF16)
    head_mask = jnp.where(same_head, 1.0, 0.0)
    sub_row = lax.broadcasted_iota(jnp.int32, (SUB, w), 0)

    st = st_scr[...]
    outs = []
    for a in range(rows // SUB):
        rs = slice(a * SUB, (a + 1) * SUB)
        o_inter = lax.dot_general(q_dec[rs].astype(BF16), st.astype(BF16),
                                  (((1,), (1,)), ((), ())), preferred_element_type=F32)
        cum_a, q_a, k_a, v_a = cum[rs], query[rs], key[rs], value[rs]
        pieces = []
        for s in range(SUB):
            decay = jnp.exp(jnp.minimum(cum_a - cum_a[s:s + 1], 0.0))
            pieces.append(jnp.where(sub_row >= s, decay * (q_a * k_a[s:s + 1]), 0.0))
        e_all = jnp.concatenate(pieces, axis=0)
        e_hi = e_all.astype(BF16)
        e_lo = (e_all - e_hi.astype(F32)).astype(BF16)
        scores = (jnp.dot(e_hi, head_ones, preferred_element_type=F32)
                  + jnp.dot(e_lo, head_ones, preferred_element_type=F32))
        o_diag = jnp.zeros((SUB, w), F32)
        for s in range(SUB):
            o_diag = o_diag + scores[s * SUB:(s + 1) * SUB] * v_a[s:s + 1]
        outs.append(o_inter + o_diag)
        k_sub = jnp.where((row >= a * SUB) & (row < (a + 1) * SUB), k_dec, 0.0).astype(BF16)
        upd_t = jnp.dot(value_t, k_sub, preferred_element_type=F32)
        st = e_tot[a * SUB:a * SUB + 1] * st + upd_t * head_mask
    st_scr[...] = st

    o = jnp.concatenate(outs, axis=0)
    ms = jnp.dot(o * o, head_mask, precision=HIGHEST,
                 preferred_element_type=F32) * (1.0 / HGRN_HEAD_DIM)
    y_hgrn = o * lax.rsqrt(ms + EPS) * onorm_ref[...] * _silu(og)
    y_ref[0, :, 0:POOL_WIDTH] = y_pool
    y_ref[0, :, POOL_WIDTH:] = y_hgrn


def _mix(rest, lb_row, onorm_row, poolw_bd, pscale_row):
    bsz, seq, _ = rest.shape
    return pl.pallas_call(
        _mix_body,
        out_shape=jax.ShapeDtypeStruct((bsz, seq, POOL_WIDTH + HGRN_WIDTH), F32),
        grid=(bsz, seq // MIX_ROWS),
        in_specs=[pl.BlockSpec((1, MIX_ROWS, REST_WIDTH), lambda b, t: (b, t, 0)),
                  _const_spec((1, HGRN_WIDTH)), _const_spec((1, HGRN_WIDTH)),
                  _const_spec((POOL_WIDTH, POOL_WIDTH)), _const_spec((1, POOL_WIDTH))],
        out_specs=pl.BlockSpec((1, MIX_ROWS, POOL_WIDTH + HGRN_WIDTH), lambda b, t: (b, t, 0)),
        scratch_shapes=[pltpu.VMEM((HGRN_WIDTH, HGRN_WIDTH), F32),
                        pltpu.VMEM((MIX_ROWS, POOL_WIDTH), F32)],
        compiler_params=pltpu.CompilerParams(dimension_semantics=("arbitrary", "arbitrary"),
                                             vmem_limit_bytes=VMEM_LIMIT),
        name="pool_hgrn",
    )(rest, lb_row, onorm_row, poolw_bd, pscale_row)


def _block_diag(blocks):
    g, c, e = blocks.shape
    out = jnp.zeros((g * c, g * e), blocks.dtype)
    for i in range(g):
        out = out.at[i * c:(i + 1) * c, i * e:(i + 1) * e].set(blocks[i])
    return out


def kernel(x, ffn1_norm, ffn1_w_gate, ffn1_w_up, ffn1_w_down, mix_norm, w_in, q_norm, k_norm,
           pool_w, pool_scale, hgrn_lb, hgrn_out_norm, w_out, ffn2_norm, ffn2_w_gate, ffn2_w_up,
           ffn2_w_down):
    bsz, seq, d = x.shape
    n = bsz * seq

    inv_freq = ROPE_THETA ** (-jnp.arange(HALF, dtype=F32) / HALF)
    ang = jnp.arange(seq).astype(F32)[:, None] * inv_freq[None, :]
    cos_t, sin_t = jnp.cos(ang).T, jnp.sin(ang).T

    lb_soft = jax.nn.softmax(hgrn_lb.astype(F32), axis=0)
    lower_bounds = jnp.concatenate([jnp.zeros_like(lb_soft[:1]), jnp.cumsum(lb_soft[:-1], axis=0)],
                                   axis=0)

    ffn1 = (ffn1_norm[:, None, :], ffn1_w_gate.astype(BF16), ffn1_w_up.astype(BF16),
            ffn1_w_down.astype(BF16))
    ffn2 = (ffn2_norm[:, None, :], ffn2_w_gate.astype(BF16), ffn2_w_up.astype(BF16),
            ffn2_w_down.astype(BF16))
    w_in_b = w_in.astype(BF16)
    w_qkv_t = jnp.swapaxes(w_in_b[:, :, :3 * ATTN_WIDTH], 1, 2)
    w_rest = w_in_b[:, :, 3 * ATTN_WIDTH:]
    w_out_b = w_out.astype(BF16)

    x2d = x.reshape(n, d)
    for l in range(DEPTH):
        x2d = _ffn(x2d, l, *ffn1)
        qt, k, vt, rest = _proj(x2d.reshape(bsz, seq, d), mix_norm[l][None, :], w_qkv_t[l], w_rest[l],
                                q_norm[l][:, None], k_norm[l][:, None], cos_t, sin_t)
        y_attn = _attn(qt, k, vt)
        y_rest = _mix(rest, lower_bounds[l][None, :], jnp.tile(hgrn_out_norm[l], 4)[None, :],
                      _block_diag(pool_w[l].astype(F32)), pool_scale[l][None, :])
        x2d = _outproj_ffn(x2d, y_attn.reshape(n, ATTN_WIDTH), y_rest.reshape(n, ATTN_WIDTH),
                           l, w_out_b, *ffn2)
    return x2d.reshape(bsz, seq, d)
```

```python
import functools

import jax
import jax.numpy as jnp
import numpy as np
from jax import lax
from jax.experimental import pallas as pl
from jax.experimental.pallas import tpu as pltpu

F32 = jnp.float32
BF16 = jnp.bfloat16
HIGHEST = lax.Precision.HIGHEST

D_MODEL = 1024
DEPTH = 4
ATTN_HEADS = 8
HEAD_DIM = 64
HALF = HEAD_DIM // 2
ATTN_WIDTH = ATTN_HEADS * HEAD_DIM
HEAD_PAIRS = ATTN_HEADS // 2
PAIR_DIM = 2 * HEAD_DIM
POOL_WINDOWS = (2, 4, 8, 16)
POOL_GROUP_DIM = 64
POOL_WIDTH = 256
HGRN_HEAD_DIM = 64
HGRN_WIDTH = 256
REST_WIDTH = POOL_WIDTH + 4 * HGRN_WIDTH
D_FF = 2816
FF_CHUNK = 256
N_FF_CHUNKS = D_FF // FF_CHUNK
MOBA_BLOCK = 256
MOBA_TOPK = 3
ROPE_THETA = 10000.0
EPS = 1e-6
NEG_INF = -1e30
LB_FLOOR = 1e-20

FFN_ROWS = 512
PROJ_ROWS = 512
MIX_ROWS = 128
SUB = 16
POOL_HALO = 16
ATTN_PAIRS = 4
ONES_ROWS = 16
MASK_BIAS = 2 * NEG_INF
LOG2_E = 1.4426950408889634
VMEM_LIMIT = 56 * 1024 * 1024


def _rms_rows(x, gain_row):
    ms = jnp.mean(x * x, axis=-1, keepdims=True)
    return x * lax.rsqrt(ms + EPS) * gain_row


def _silu(x):
    return x * jax.nn.sigmoid(x)


def _swiglu_half_step(x, gain_row, wg_ref, wu_ref, wd_ref):
    h = _rms_rows(x, gain_row).astype(BF16)
    acc = jnp.zeros_like(x)
    for c in range(N_FF_CHUNKS):
        cols = slice(c * FF_CHUNK, (c + 1) * FF_CHUNK)
        g = jnp.dot(h, wg_ref[0, :, cols], preferred_element_type=F32)
        u = jnp.dot(h, wu_ref[0, :, cols], preferred_element_type=F32)
        acc = acc + jnp.dot((_silu(g) * u).astype(BF16), wd_ref[0, cols, :],
                            preferred_element_type=F32)
    return x + 0.5 * acc


def _ffn_body(x_ref, gain_ref, wg_ref, wu_ref, wd_ref, o_ref):
    o_ref[...] = _swiglu_half_step(x_ref[...], gain_ref[0], wg_ref, wu_ref, wd_ref)


def _outproj_ffn_body(x_ref, ya_ref, yr_ref, wo_ref, gain_ref, wg_ref, wu_ref, wd_ref, o_ref):
    x = x_ref[...]
    x = x + jnp.dot(ya_ref[...].astype(BF16), wo_ref[0, :ATTN_WIDTH, :], preferred_element_type=F32)
    x = x + jnp.dot(yr_ref[...].astype(BF16), wo_ref[0, ATTN_WIDTH:, :], preferred_element_type=F32)
    o_ref[...] = _swiglu_half_step(x, gain_ref[0], wg_ref, wu_ref, wd_ref)


def _const_spec(shape):
    return pl.BlockSpec(shape, lambda *_: (0,) * len(shape), pipeline_mode=pl.Buffered(1))


def _layer_spec(shape, layer):
    return pl.BlockSpec((1,) + shape, lambda *_: (layer,) + (0,) * len(shape),
                        pipeline_mode=pl.Buffered(1))


def _ffn_weight_specs(layer):
    return [_layer_spec((1, D_MODEL), layer), _layer_spec((D_MODEL, D_FF), layer),
            _layer_spec((D_MODEL, D_FF), layer), _layer_spec((D_FF, D_MODEL), layer)]


def _ffn(x2d, layer, gains, wg, wu, wd):
    n = x2d.shape[0]
    row_spec = pl.BlockSpec((FFN_ROWS, D_MODEL), lambda i: (i, 0))
    return pl.pallas_call(
        _ffn_body,
        out_shape=jax.ShapeDtypeStruct((n, D_MODEL), F32),
        grid=(n // FFN_ROWS,),
        in_specs=[row_spec] + _ffn_weight_specs(layer),
        out_specs=row_spec,
        compiler_params=pltpu.CompilerParams(dimension_semantics=("arbitrary",),
                                             vmem_limit_bytes=VMEM_LIMIT),
        name="ffn",
    )(x2d, gains, wg, wu, wd)


def _outproj_ffn(x2d, ya, yr, layer, wo, gains, wg, wu, wd):
    n = x2d.shape[0]
    row_spec = pl.BlockSpec((FFN_ROWS, D_MODEL), lambda i: (i, 0))
    half_spec = pl.BlockSpec((FFN_ROWS, ATTN_WIDTH), lambda i: (i, 0))
    return pl.pallas_call(
        _outproj_ffn_body,
        out_shape=jax.ShapeDtypeStruct((n, D_MODEL), F32),
        grid=(n // FFN_ROWS,),
        in_specs=[row_spec, half_spec, half_spec, _layer_spec((D_MODEL, D_MODEL), layer)]
        + _ffn_weight_specs(layer),
        out_specs=row_spec,
        compiler_params=pltpu.CompilerParams(dimension_semantics=("arbitrary",),
                                             vmem_limit_bytes=VMEM_LIMIT),
        name="outproj_ffn",
    )(x2d, ya, yr, wo, gains, wg, wu, wd)


def _proj_body(x_ref, gain_ref, wt_ref, wr_ref, qn_ref, kn_ref, cos_ref, sin_ref,
               qt_ref, k_ref, vt_ref, rest_ref):
    h = _rms_rows(x_ref[0], gain_ref[...]).astype(BF16)
    rest_ref[0] = jnp.dot(h, wr_ref[...], preferred_element_type=F32)
    qkvt = lax.dot_general(wt_ref[...], h, (((1,), (1,)), ((), ())), preferred_element_type=F32)
    cos = cos_ref[...]
    sin = sin_ref[...]

    def norm_rope(t, gain_col):
        ms = jnp.mean(t * t, axis=0, keepdims=True)
        tn = t * lax.rsqrt(ms + EPS) * gain_col
        x1, x2 = tn[:HALF], tn[HALF:]
        return jnp.concatenate([x1 * cos - x2 * sin, x2 * cos + x1 * sin], axis=0)

    n_blk = PROJ_ROWS // MOBA_BLOCK
    for p in range(HEAD_PAIRS):
        lo = p * PAIR_DIM
        q_pair = jnp.concatenate(
            [norm_rope(qkvt[lo + e * HEAD_DIM:lo + (e + 1) * HEAD_DIM], qn_ref[...]) for e in range(2)],
            axis=0)
        k_pair = jnp.concatenate(
            [norm_rope(qkvt[ATTN_WIDTH + lo + e * HEAD_DIM:ATTN_WIDTH + lo + (e + 1) * HEAD_DIM],
                       kn_ref[...]) for e in range(2)], axis=0)
        v_pair = qkvt[2 * ATTN_WIDTH + lo:2 * ATTN_WIDTH + lo + PAIR_DIM].astype(BF16)
        k_ref[0, :, lo:lo + PAIR_DIM] = k_pair.T.astype(BF16)
        for b in range(n_blk):
            qt_ref[0, p, b] = q_pair[:, b * MOBA_BLOCK:(b + 1) * MOBA_BLOCK]
            vt_ref[0, p, b] = v_pair[:, b * MOBA_BLOCK:(b + 1) * MOBA_BLOCK]


def _proj(x3d, gain_row, wt, wr, qn_col, kn_col, cos_t, sin_t):
    bsz, seq, _ = x3d.shape
    nb = seq // MOBA_BLOCK
    n_blk = PROJ_ROWS // MOBA_BLOCK
    blk_spec = pl.BlockSpec((1, HEAD_PAIRS, n_blk, PAIR_DIM, MOBA_BLOCK), lambda b, i: (b, 0, i, 0, 0))
    return pl.pallas_call(
        _proj_body,
        out_shape=(jax.ShapeDtypeStruct((bsz, HEAD_PAIRS, nb, PAIR_DIM, MOBA_BLOCK), F32),
                   jax.ShapeDtypeStruct((bsz, seq, ATTN_WIDTH), BF16),
                   jax.ShapeDtypeStruct((bsz, HEAD_PAIRS, nb, PAIR_DIM, MOBA_BLOCK), BF16),
                   jax.ShapeDtypeStruct((bsz, seq, REST_WIDTH), F32)),
        grid=(bsz, seq // PROJ_ROWS),
        in_specs=[pl.BlockSpec((1, PROJ_ROWS, D_MODEL), lambda b, i: (b, i, 0)),
                  _const_spec((1, D_MODEL)),
                  _const_spec((3 * ATTN_WIDTH, D_MODEL)),
                  _const_spec((D_MODEL, REST_WIDTH)),
                  _const_spec((HEAD_DIM, 1)), _const_spec((HEAD_DIM, 1)),
                  pl.BlockSpec((HALF, PROJ_ROWS), lambda b, i: (0, i)),
                  pl.BlockSpec((HALF, PROJ_ROWS), lambda b, i: (0, i))],
        out_specs=(blk_spec,
                   pl.BlockSpec((1, PROJ_ROWS, ATTN_WIDTH), lambda b, i: (b, i, 0)),
                   blk_spec,
                   pl.BlockSpec((1, PROJ_ROWS, REST_WIDTH), lambda b, i: (b, i, 0))),
        compiler_params=pltpu.CompilerParams(dimension_semantics=("arbitrary", "arbitrary"),
                                             vmem_limit_bytes=VMEM_LIMIT),
        name="proj",
    )(x3d, gain_row, wt, wr, qn_col, kn_col, cos_t, sin_t)


def _attn_body(qt_ref, k_ref, vt_ref, y_ref, kmean_scr, bias_scr, *s_scrs, nb):
    i = pl.program_id(2)
    width = ATTN_PAIRS * PAIR_DIM

    @pl.when(i == 0)
    def _():
        for j in range(nb):
            kb = k_ref[0, j * MOBA_BLOCK:(j + 1) * MOBA_BLOCK, :].astype(F32)
            kmean_scr[j:j + 1, :] = jnp.sum(kb, axis=0, keepdims=True) * (1.0 / MOBA_BLOCK)

    kmean = kmean_scr[...]
    lane = lax.broadcasted_iota(jnp.int32, (nb, width), 1)
    qrow = lax.broadcasted_iota(jnp.int32, (PAIR_DIM, MOBA_BLOCK), 0)
    blk = lax.broadcasted_iota(jnp.int32, (nb, MOBA_BLOCK), 0)
    valid = blk < i

    heads = [(p, e) for p in range(ATTN_PAIRS) for e in range(2)]
    q_heads = []
    for h, (p, e) in enumerate(heads):
        qt = qt_ref[0, p, 0]
        lo = p * PAIR_DIM + e * HEAD_DIM
        km = jnp.where((lane >= lo) & (lane < lo + HEAD_DIM), kmean, 0.0)
        gate = jnp.dot(km[:, p * PAIR_DIM:(p + 1) * PAIR_DIM], qt, precision=HIGHEST,
                       preferred_element_type=F32)
        sel = jnp.zeros((nb, MOBA_BLOCK), F32)
        for _ in range(MOBA_TOPK):
            cand = jnp.where(valid & (sel == 0.0), gate, -jnp.inf)
            best = jnp.max(cand, axis=0, keepdims=True)
            is_best = (cand == best) & (best > -jnp.inf)
            first = jnp.min(jnp.where(is_best, blk, nb), axis=0, keepdims=True)
            sel = jnp.where(blk == first, 1.0, sel)
        bias_scr[h] = jnp.where(sel > 0.0, 0.0, MASK_BIAS)
        in_rows = (qrow >= e * HEAD_DIM) & (qrow < (e + 1) * HEAD_DIM)
        q_heads.append(jnp.where(in_rows, qt * (HEAD_DIM ** -0.5 * LOG2_E), 0.0).astype(BF16))

    ones_rows = jnp.ones((ONES_ROWS, MOBA_BLOCK), BF16)

    def values_ext(j, h):
        p, e = heads[h]
        return jnp.concatenate([vt_ref[0, p, j][e * HEAD_DIM:(e + 1) * HEAD_DIM, :], ones_rows], axis=0)

    def scores(j, h):
        p = heads[h][0]
        kj = k_ref[0, pl.ds(pl.multiple_of(j * MOBA_BLOCK, MOBA_BLOCK), MOBA_BLOCK),
                   p * PAIR_DIM:(p + 1) * PAIR_DIM]
        return jnp.dot(kj, q_heads[h], preferred_element_type=F32)

    n_heads = len(heads)
    init = []
    for h in range(n_heads):
        s = scores(0, h)
        s_scrs[h][0] = s
        init += [jnp.full((1, MOBA_BLOCK), NEG_INF, F32), jnp.max(s, axis=0, keepdims=True),
                 jnp.zeros((HEAD_DIM + ONES_ROWS, MOBA_BLOCK), F32)]

    def body(j, carry):
        slot = j & 1
        out = []
        for h in range(n_heads):
            m, cmax, acc = carry[3 * h:3 * h + 3]
            brow = bias_scr[h, pl.ds(j, 1), :]
            m_new = jnp.maximum(m, cmax + brow)
            p = jnp.exp2(s_scrs[h][slot] + (brow - m_new)).astype(BF16)
            acc = jnp.exp2(m - m_new) * acc + jnp.dot(values_ext(j, h), p, preferred_element_type=F32)
            s_next = scores(j + 1, h)
            s_scrs[h][1 - slot] = s_next
            out += [m_new, jnp.max(s_next, axis=0, keepdims=True), acc]
        return tuple(out)

    fin = lax.fori_loop(0, i, body, tuple(init))

    key_pos = lax.broadcasted_iota(jnp.int32, (MOBA_BLOCK, MOBA_BLOCK), 0)
    q_pos = lax.broadcasted_iota(jnp.int32, (MOBA_BLOCK, MOBA_BLOCK), 1)
    causal_bias = jnp.where(key_pos <= q_pos, 0.0, MASK_BIAS)
    outs = []
    for h in range(n_heads):
        m, _, acc = fin[3 * h:3 * h + 3]
        s = s_scrs[h][i & 1] + causal_bias
        m_new = jnp.maximum(m, jnp.max(s, axis=0, keepdims=True))
        p = jnp.exp2(s - m_new).astype(BF16)
        acc = jnp.exp2(m - m_new) * acc + jnp.dot(values_ext(i, h), p, preferred_element_type=F32)
        outs.append(acc[:HEAD_DIM] / acc[HEAD_DIM:HEAD_DIM + 1])
    for p in range(ATTN_PAIRS):
        o_pair = jnp.concatenate(outs[2 * p:2 * p + 2], axis=0)
        y_ref[0, :, p * PAIR_DIM:(p + 1) * PAIR_DIM] = o_pair.T


def _attn(qt, k, vt):
    bsz, _, nb, _, _ = qt.shape
    seq = nb * MOBA_BLOCK
    width = ATTN_PAIRS * PAIR_DIM
    n_heads = 2 * ATTN_PAIRS
    return pl.pallas_call(
        functools.partial(_attn_body, nb=nb),
        out_shape=jax.ShapeDtypeStruct((bsz, seq, ATTN_WIDTH), F32),
        grid=(bsz, HEAD_PAIRS // ATTN_PAIRS, nb),
        in_specs=[pl.BlockSpec((1, ATTN_PAIRS, 1, PAIR_DIM, MOBA_BLOCK), lambda b, p, i: (b, p, i, 0, 0)),
                  pl.BlockSpec((1, seq, width), lambda b, p, i: (b, 0, p)),
                  pl.BlockSpec((1, ATTN_PAIRS, nb, PAIR_DIM, MOBA_BLOCK),
                               lambda b, p, i: (b, p, 0, 0, 0))],
        out_specs=pl.BlockSpec((1, MOBA_BLOCK, width), lambda b, p, i: (b, i, p)),
        scratch_shapes=[pltpu.VMEM((nb, width), F32),
                        pltpu.VMEM((n_heads, nb, MOBA_BLOCK), F32)]
        + [pltpu.VMEM((2, MOBA_BLOCK, MOBA_BLOCK), F32) for _ in range(n_heads)],
        compiler_params=pltpu.CompilerParams(
            dimension_semantics=("arbitrary", "arbitrary", "arbitrary"),
            vmem_limit_bytes=VMEM_LIMIT),
        name="moba_attn",
    )(qt, k, vt)


def _mix_body(r_ref, lb_ref, onorm_ref, poolw_ref, pscale_ref, tri_ref, hones_ref, hmask_ref,
              y_ref, st_scr, uprev_scr):
    t = pl.program_id(1)

    @pl.when(t == 0)
    def _():
        st_scr[...] = jnp.zeros_like(st_scr)
        uprev_scr[...] = jnp.zeros_like(uprev_scr)

    rows, w = MIX_ROWS, HGRN_WIDTH
    u = r_ref[0, :, 0:POOL_WIDTH]
    qh = r_ref[0, :, POOL_WIDTH:POOL_WIDTH + w]
    z = r_ref[0, :, POOL_WIDTH + w:POOL_WIDTH + 2 * w]
    value = r_ref[0, :, POOL_WIDTH + 2 * w:POOL_WIDTH + 3 * w]
    og = r_ref[0, :, POOL_WIDTH + 3 * w:POOL_WIDTH + 4 * w]

    lane = lax.broadcasted_iota(jnp.int32, (rows, w), 1)
    row = lax.broadcasted_iota(jnp.int32, (rows, w), 0)

    uext = jnp.concatenate([uprev_scr[...], u], axis=0)
    sums = []
    acc = uext
    for shift in (1, 2, 4, 8):
        acc = acc + pltpu.roll(acc, shift, 0)
        sums.append(acc[POOL_HALO:])
    in_g0, in_g01, in_g012 = (lane < g * POOL_GROUP_DIM for g in (1, 2, 3))
    win_sum = jnp.where(in_g0, sums[0], jnp.where(in_g01, sums[1], jnp.where(in_g012, sums[2], sums[3])))
    window = jnp.where(in_g0, POOL_WINDOWS[0],
                       jnp.where(in_g01, POOL_WINDOWS[1],
                                 jnp.where(in_g012, POOL_WINDOWS[2], POOL_WINDOWS[3])))
    count = jnp.minimum(t * rows + row + 1, window).astype(F32)
    diff = win_sum / count - u
    y_pool = jnp.dot(diff.astype(BF16), poolw_ref[...], preferred_element_type=F32) * pscale_ref[...]
    uprev_scr[...] = u[rows - POOL_HALO:]

    lb = lb_ref[...]
    log_sig = jnp.minimum(z, 0.0) - jnp.log1p(jnp.exp(-jnp.abs(z)))
    a_term = jnp.log(jnp.maximum(lb, LB_FLOOR))
    b_term = jnp.log1p(-lb) + log_sig
    log_f = jnp.maximum(a_term, b_term) + jnp.log1p(jnp.exp(-jnp.abs(a_term - b_term)))
    key = (1.0 - lb) * jax.nn.sigmoid(-z)
    query = _silu(qh) * (HGRN_HEAD_DIM ** -0.5)

    tri = tri_ref[...]
    f_hi = log_f.astype(BF16)
    f_rest = log_f - f_hi.astype(F32)
    f_mid = f_rest.astype(BF16)
    f_lo = (f_rest - f_mid.astype(F32)).astype(BF16)
    cum = (jnp.dot(tri, f_hi, preferred_element_type=F32) + jnp.dot(tri, f_mid, preferred_element_type=F32)
           + jnp.dot(tri, f_lo, preferred_element_type=F32))
    n_sub = rows // SUB
    last = [cum[(a + 1) * SUB - 1:(a + 1) * SUB] for a in range(n_sub)]
    tot = jnp.concatenate([jnp.broadcast_to(r, (SUB, w)) for r in last], axis=0)
    k_dec = (key * jnp.exp(tot - cum)).astype(BF16)
    q_dec = (query * jnp.exp(cum)).astype(BF16)
    value_b = value.astype(BF16)

    head_ones = hones_ref[...]
    head_mask = hmask_ref[...]
    sub_row = lax.broadcasted_iota(jnp.int32, (SUB, w), 0)
    half_row = lax.broadcasted_iota(jnp.int32, (SUB // 2, w), 0) + SUB // 2

    st = st_scr[...]
    outs = []
    for a in range(n_sub):
        rs = slice(a * SUB, (a + 1) * SUB)
        o_inter = lax.dot_general(q_dec[rs], st.astype(BF16), (((1,), (1,)), ((), ())),
                                  preferred_element_type=F32)
        cum_a, q_a, k_a, v_a = cum[rs], query[rs], key[rs], value[rs]
        pieces = []
        for s in range(SUB):
            if s < SUB // 2:
                decay = jnp.exp(jnp.minimum(cum_a - cum_a[s:s + 1], 0.0))
                pieces.append(jnp.where(sub_row >= s, decay * (q_a * k_a[s:s + 1]), 0.0))
            else:
                decay = jnp.exp(jnp.minimum(cum_a[SUB // 2:] - cum_a[s:s + 1], 0.0))
                pieces.append(jnp.where(half_row >= s, decay * (q_a[SUB // 2:] * k_a[s:s + 1]), 0.0))
        e_all = jnp.concatenate(pieces, axis=0).astype(BF16)
        scores = jnp.dot(e_all, head_ones, preferred_element_type=F32)
        o_top = jnp.zeros((SUB // 2, w), F32)
        o_bot = jnp.zeros((SUB // 2, w), F32)
        for s in range(SUB):
            if s < SUB // 2:
                o_top = o_top + scores[s * SUB:s * SUB + SUB // 2] * v_a[s:s + 1]
                o_bot = o_bot + scores[s * SUB + SUB // 2:(s + 1) * SUB] * v_a[s:s + 1]
            else:
                base = (SUB // 2) * SUB + (s - SUB // 2) * (SUB // 2)
                o_bot = o_bot + scores[base:base + SUB // 2] * v_a[s:s + 1]
        outs.append(o_inter + jnp.concatenate([o_top, o_bot], axis=0))
        upd_t = lax.dot_general(value_b[rs], k_dec[rs], (((0,), (0,)), ((), ())),
                                preferred_element_type=F32)
        st = jnp.exp(last[a]) * st + upd_t * head_mask
    st_scr[...] = st

    o = jnp.concatenate(outs, axis=0)
    sq = o * o
    sq_hi = sq.astype(BF16)
    sq_lo = (sq - sq_hi.astype(F32)).astype(BF16)
    ms = (jnp.dot(sq_hi, head_ones, preferred_element_type=F32)
          + jnp.dot(sq_lo, head_ones, preferred_element_type=F32)) * (1.0 / HGRN_HEAD_DIM)
    y_hgrn = o * lax.rsqrt(ms + EPS) * onorm_ref[...] * _silu(og)
    y_ref[0, :, 0:POOL_WIDTH] = y_pool
    y_ref[0, :, POOL_WIDTH:] = y_hgrn


def _mix_constants():
    r = np.arange(MIX_ROWS)
    tri = ((r[:, None] // SUB == r[None, :] // SUB) & (r[None, :] <= r[:, None])).astype(np.float32)
    c = np.arange(HGRN_WIDTH) // HGRN_HEAD_DIM
    same_head = (c[:, None] == c[None, :]).astype(np.float32)
    return jnp.asarray(tri, BF16), jnp.asarray(same_head, BF16), jnp.asarray(same_head, F32)


def _mix(rest, lb_row, onorm_row, poolw_bd, pscale_row):
    bsz, seq, _ = rest.shape
    tri, head_ones, head_mask = _mix_constants()
    return pl.pallas_call(
        _mix_body,
        out_shape=jax.ShapeDtypeStruct((bsz, seq, POOL_WIDTH + HGRN_WIDTH), F32),
        grid=(bsz, seq // MIX_ROWS),
        in_specs=[pl.BlockSpec((1, MIX_ROWS, REST_WIDTH), lambda b, t: (b, t, 0)),
                  _const_spec((1, HGRN_WIDTH)), _const_spec((1, HGRN_WIDTH)),
                  _const_spec((POOL_WIDTH, POOL_WIDTH)), _const_spec((1, POOL_WIDTH)),
                  _const_spec((MIX_ROWS, MIX_ROWS)), _const_spec((HGRN_WIDTH, HGRN_WIDTH)),
                  _const_spec((HGRN_WIDTH, HGRN_WIDTH))],
        out_specs=pl.BlockSpec((1, MIX_ROWS, POOL_WIDTH + HGRN_WIDTH), lambda b, t: (b, t, 0)),
        scratch_shapes=[pltpu.VMEM((HGRN_WIDTH, HGRN_WIDTH), F32),
                        pltpu.VMEM((POOL_HALO, POOL_WIDTH), F32)],
        compiler_params=pltpu.CompilerParams(dimension_semantics=("arbitrary", "arbitrary"),
                                             vmem_limit_bytes=VMEM_LIMIT),
        name="pool_hgrn",
    )(rest, lb_row, onorm_row, poolw_bd, pscale_row, tri, head_ones, head_mask)


def _block_diag(blocks):
    g, c, e = blocks.shape
    out = jnp.zeros((g * c, g * e), blocks.dtype)
    for i in range(g):
        out = out.at[i * c:(i + 1) * c, i * e:(i + 1) * e].set(blocks[i])
    return out


def kernel(x, ffn1_norm, ffn1_w_gate, ffn1_w_up, ffn1_w_down, mix_norm, w_in, q_norm, k_norm,
           pool_w, pool_scale, hgrn_lb, hgrn_out_norm, w_out, ffn2_norm, ffn2_w_gate, ffn2_w_up,
           ffn2_w_down):
    bsz, seq, d = x.shape
    n = bsz * seq

    inv_freq = ROPE_THETA ** (-jnp.arange(HALF, dtype=F32) / HALF)
    ang = jnp.arange(seq).astype(F32)[:, None] * inv_freq[None, :]
    cos_t, sin_t = jnp.cos(ang).T, jnp.sin(ang).T

    lb_soft = jax.nn.softmax(hgrn_lb.astype(F32), axis=0)
    lower_bounds = jnp.concatenate([jnp.zeros_like(lb_soft[:1]), jnp.cumsum(lb_soft[:-1], axis=0)],
                                   axis=0)

    ffn1 = (ffn1_norm[:, None, :], ffn1_w_gate.astype(BF16), ffn1_w_up.astype(BF16),
            ffn1_w_down.astype(BF16))
    ffn2 = (ffn2_norm[:, None, :], ffn2_w_gate.astype(BF16), ffn2_w_up.astype(BF16),
            ffn2_w_down.astype(BF16))
    w_in_b = w_in.astype(BF16)
    w_qkv_t = jnp.swapaxes(w_in_b[:, :, :3 * ATTN_WIDTH], 1, 2)
    w_rest = w_in_b[:, :, 3 * ATTN_WIDTH:]
    w_out_b = w_out.astype(BF16)

    x2d = x.reshape(n, d)
    for l in range(DEPTH):
        x2d = _ffn(x2d, l, *ffn1)
        qt, k, vt, rest = _proj(x2d.reshape(bsz, seq, d), mix_norm[l][None, :], w_qkv_t[l], w_rest[l],
                                q_norm[l][:, None], k_norm[l][:, None], cos_t, sin_t)
        y_attn = _attn(qt, k, vt)
        y_rest = _mix(rest, lower_bounds[l][None, :], jnp.tile(hgrn_out_norm[l], 4)[None, :],
                      _block_diag(pool_w[l]).astype(BF16), pool_scale[l][None, :])
        x2d = _outproj_ffn(x2d, y_attn.reshape(n, ATTN_WIDTH), y_rest.reshape(n, ATTN_WIDTH),
                           l, w_out_b, *ffn2)
    return x2d.reshape(bsz, seq, d)
```

```python
import functools

import jax
import jax.numpy as jnp
import numpy as np
from jax import lax
from jax.experimental import pallas as pl
from jax.experimental.pallas import tpu as pltpu

F32 = jnp.float32
BF16 = jnp.bfloat16
HIGHEST = lax.Precision.HIGHEST

D_MODEL = 1024
DEPTH = 4
ATTN_HEADS = 8
HEAD_DIM = 64
HALF = HEAD_DIM // 2
ATTN_WIDTH = ATTN_HEADS * HEAD_DIM
HEAD_PAIRS = ATTN_HEADS // 2
PAIR_DIM = 2 * HEAD_DIM
POOL_WINDOWS = (2, 4, 8, 16)
POOL_GROUP_DIM = 64
POOL_WIDTH = 256
HGRN_HEAD_DIM = 64
HGRN_WIDTH = 256
REST_WIDTH = POOL_WIDTH + 4 * HGRN_WIDTH
D_FF = 2816
FF_CHUNK = 256
N_FF_CHUNKS = D_FF // FF_CHUNK
MOBA_BLOCK = 256
MOBA_TOPK = 3
ROPE_THETA = 10000.0
EPS = 1e-6
NEG_INF = -1e30
LB_FLOOR = 1e-20

FFN_ROWS = 512
PROJ_ROWS = 512
MIX_ROWS = 128
SUB = 16
POOL_HALO = 16
ATTN_PAIRS = 4
ATTN_UNROLL = 4
ONES_ROWS = 16
MASK_BIAS = 2 * NEG_INF
LOG2_E = 1.4426950408889634
VMEM_LIMIT = 56 * 1024 * 1024


def _rms_rows(x, gain_row):
    ms = jnp.mean(x * x, axis=-1, keepdims=True)
    return x * lax.rsqrt(ms + EPS) * gain_row


def _silu(x):
    return x * jax.nn.sigmoid(x)


def _swiglu_half_step(x, gain_row, wg_ref, wu_ref, wd_ref):
    h = _rms_rows(x, gain_row).astype(BF16)
    acc = jnp.zeros_like(x)
    for c in range(N_FF_CHUNKS):
        cols = slice(c * FF_CHUNK, (c + 1) * FF_CHUNK)
        g = jnp.dot(h, wg_ref[0, :, cols], preferred_element_type=F32)
        u = jnp.dot(h, wu_ref[0, :, cols], preferred_element_type=F32)
        acc = acc + jnp.dot((_silu(g) * u).astype(BF16), wd_ref[0, cols, :],
                            preferred_element_type=F32)
    return x + 0.5 * acc


def _ffn_body(x_ref, gain_ref, wg_ref, wu_ref, wd_ref, o_ref):
    o_ref[...] = _swiglu_half_step(x_ref[...], gain_ref[0], wg_ref, wu_ref, wd_ref)


def _outproj_ffn_body(x_ref, ya_ref, yr_ref, wo_ref, gain_ref, wg_ref, wu_ref, wd_ref, o_ref):
    x = x_ref[...]
    x = x + jnp.dot(ya_ref[...].astype(BF16), wo_ref[0, :ATTN_WIDTH, :], preferred_element_type=F32)
    x = x + jnp.dot(yr_ref[...].astype(BF16), wo_ref[0, ATTN_WIDTH:, :], preferred_element_type=F32)
    o_ref[...] = _swiglu_half_step(x, gain_ref[0], wg_ref, wu_ref, wd_ref)


def _const_spec(shape):
    return pl.BlockSpec(shape, lambda *_: (0,) * len(shape), pipeline_mode=pl.Buffered(1))


def _layer_spec(shape, layer):
    return pl.BlockSpec((1,) + shape, lambda *_: (layer,) + (0,) * len(shape),
                        pipeline_mode=pl.Buffered(1))


def _ffn_weight_specs(layer):
    return [_layer_spec((1, D_MODEL), layer), _layer_spec((D_MODEL, D_FF), layer),
            _layer_spec((D_MODEL, D_FF), layer), _layer_spec((D_FF, D_MODEL), layer)]


def _ffn(x2d, layer, gains, wg, wu, wd):
    n = x2d.shape[0]
    row_spec = pl.BlockSpec((FFN_ROWS, D_MODEL), lambda i: (i, 0))
    return pl.pallas_call(
        _ffn_body,
        out_shape=jax.ShapeDtypeStruct((n, D_MODEL), F32),
        grid=(n // FFN_ROWS,),
        in_specs=[row_spec] + _ffn_weight_specs(layer),
        out_specs=row_spec,
        compiler_params=pltpu.CompilerParams(dimension_semantics=("arbitrary",),
                                             vmem_limit_bytes=VMEM_LIMIT),
        name="ffn",
    )(x2d, gains, wg, wu, wd)


def _outproj_ffn(x2d, ya, yr, layer, wo, gains, wg, wu, wd):
    n = x2d.shape[0]
    row_spec = pl.BlockSpec((FFN_ROWS, D_MODEL), lambda i: (i, 0))
    half_spec = pl.BlockSpec((FFN_ROWS, ATTN_WIDTH), lambda i: (i, 0))
    return pl.pallas_call(
        _outproj_ffn_body,
        out_shape=jax.ShapeDtypeStruct((n, D_MODEL), F32),
        grid=(n // FFN_ROWS,),
        in_specs=[row_spec, half_spec, half_spec, _layer_spec((D_MODEL, D_MODEL), layer)]
        + _ffn_weight_specs(layer),
        out_specs=row_spec,
        compiler_params=pltpu.CompilerParams(dimension_semantics=("arbitrary",),
                                             vmem_limit_bytes=VMEM_LIMIT),
        name="outproj_ffn",
    )(x2d, ya, yr, wo, gains, wg, wu, wd)


def _proj_body(x_ref, gain_ref, wt_ref, wr_ref, qn_ref, kn_ref, cos_ref, sin_ref,
               qt_ref, k_ref, vt_ref, rest_ref):
    h = _rms_rows(x_ref[0], gain_ref[...]).astype(BF16)
    rest_ref[0] = jnp.dot(h, wr_ref[...], preferred_element_type=F32)
    qkvt = lax.dot_general(wt_ref[...], h, (((1,), (1,)), ((), ())), preferred_element_type=F32)
    cos = cos_ref[...]
    sin = sin_ref[...]

    def norm_rope(t, gain_col):
        ms = jnp.mean(t * t, axis=0, keepdims=True)
        tn = t * lax.rsqrt(ms + EPS) * gain_col
        x1, x2 = tn[:HALF], tn[HALF:]
        return jnp.concatenate([x1 * cos - x2 * sin, x2 * cos + x1 * sin], axis=0)

    n_blk = PROJ_ROWS // MOBA_BLOCK
    for p in range(HEAD_PAIRS):
        lo = p * PAIR_DIM
        q_pair = jnp.concatenate(
            [norm_rope(qkvt[lo + e * HEAD_DIM:lo + (e + 1) * HEAD_DIM], qn_ref[...]) for e in range(2)],
            axis=0)
        k_pair = jnp.concatenate(
            [norm_rope(qkvt[ATTN_WIDTH + lo + e * HEAD_DIM:ATTN_WIDTH + lo + (e + 1) * HEAD_DIM],
                       kn_ref[...]) for e in range(2)], axis=0)
        v_pair = qkvt[2 * ATTN_WIDTH + lo:2 * ATTN_WIDTH + lo + PAIR_DIM].astype(BF16)
        k_ref[0, :, lo:lo + PAIR_DIM] = k_pair.T.astype(BF16)
        for b in range(n_blk):
            qt_ref[0, p, b] = q_pair[:, b * MOBA_BLOCK:(b + 1) * MOBA_BLOCK]
            vt_ref[0, p, b] = v_pair[:, b * MOBA_BLOCK:(b + 1) * MOBA_BLOCK]


def _proj(x3d, gain_row, wt, wr, qn_col, kn_col, cos_t, sin_t):
    bsz, seq, _ = x3d.shape
    nb = seq // MOBA_BLOCK
    n_blk = PROJ_ROWS // MOBA_BLOCK
    blk_spec = pl.BlockSpec((1, HEAD_PAIRS, n_blk, PAIR_DIM, MOBA_BLOCK), lambda b, i: (b, 0, i, 0, 0))
    return pl.pallas_call(
        _proj_body,
        out_shape=(jax.ShapeDtypeStruct((bsz, HEAD_PAIRS, nb, PAIR_DIM, MOBA_BLOCK), F32),
                   jax.ShapeDtypeStruct((bsz, seq, ATTN_WIDTH), BF16),
                   jax.ShapeDtypeStruct((bsz, HEAD_PAIRS, nb, PAIR_DIM, MOBA_BLOCK), BF16),
                   jax.ShapeDtypeStruct((bsz, seq, REST_WIDTH), F32)),
        grid=(bsz, seq // PROJ_ROWS),
        in_specs=[pl.BlockSpec((1, PROJ_ROWS, D_MODEL), lambda b, i: (b, i, 0)),
                  _const_spec((1, D_MODEL)),
                  _const_spec((3 * ATTN_WIDTH, D_MODEL)),
                  _const_spec((D_MODEL, REST_WIDTH)),
                  _const_spec((HEAD_DIM, 1)), _const_spec((HEAD_DIM, 1)),
                  pl.BlockSpec((HALF, PROJ_ROWS), lambda b, i: (0, i)),
                  pl.BlockSpec((HALF, PROJ_ROWS), lambda b, i: (0, i))],
        out_specs=(blk_spec,
                   pl.BlockSpec((1, PROJ_ROWS, ATTN_WIDTH), lambda b, i: (b, i, 0)),
                   blk_spec,
                   pl.BlockSpec((1, PROJ_ROWS, REST_WIDTH), lambda b, i: (b, i, 0))),
        compiler_params=pltpu.CompilerParams(dimension_semantics=("arbitrary", "arbitrary"),
                                             vmem_limit_bytes=VMEM_LIMIT),
        name="proj",
    )(x3d, gain_row, wt, wr, qn_col, kn_col, cos_t, sin_t)


def _attn_body(qt_ref, k_ref, vt_ref, y_ref, kmean_scr, bias_scr, *s_scrs, nb):
    i = pl.program_id(2)
    width = ATTN_PAIRS * PAIR_DIM

    @pl.when(i == 0)
    def _():
        for j in range(nb):
            kb = k_ref[0, j * MOBA_BLOCK:(j + 1) * MOBA_BLOCK, :].astype(F32)
            kmean_scr[j:j + 1, :] = jnp.sum(kb, axis=0, keepdims=True) * (1.0 / MOBA_BLOCK)

    kmean = kmean_scr[...]
    lane = lax.broadcasted_iota(jnp.int32, (nb, width), 1)
    qrow = lax.broadcasted_iota(jnp.int32, (PAIR_DIM, MOBA_BLOCK), 0)
    blk = lax.broadcasted_iota(jnp.int32, (nb, MOBA_BLOCK), 0)
    valid = blk < i

    heads = [(p, e) for p in range(ATTN_PAIRS) for e in range(2)]
    q_heads = []
    for h, (p, e) in enumerate(heads):
        qt = qt_ref[0, p, 0]
        lo = p * PAIR_DIM + e * HEAD_DIM
        km = jnp.where((lane >= lo) & (lane < lo + HEAD_DIM), kmean, 0.0)
        gate = jnp.dot(km[:, p * PAIR_DIM:(p + 1) * PAIR_DIM], qt, precision=HIGHEST,
                       preferred_element_type=F32)
        sel = jnp.zeros((nb, MOBA_BLOCK), F32)
        for _ in range(MOBA_TOPK):
            cand = jnp.where(valid & (sel == 0.0), gate, -jnp.inf)
            best = jnp.max(cand, axis=0, keepdims=True)
            is_best = (cand == best) & (best > -jnp.inf)
            first = jnp.min(jnp.where(is_best, blk, nb), axis=0, keepdims=True)
            sel = jnp.where(blk == first, 1.0, sel)
        bias_scr[h] = jnp.where(sel > 0.0, 0.0, MASK_BIAS)
        in_rows = (qrow >= e * HEAD_DIM) & (qrow < (e + 1) * HEAD_DIM)
        q_heads.append(jnp.where(in_rows, qt * (HEAD_DIM ** -0.5 * LOG2_E), 0.0).astype(BF16))

    ones_rows = jnp.ones((ONES_ROWS, MOBA_BLOCK), BF16)

    def values_ext(j, h):
        p, e = heads[h]
        return jnp.concatenate([vt_ref[0, p, j][e * HEAD_DIM:(e + 1) * HEAD_DIM, :], ones_rows], axis=0)

    def scores(j, h):
        p = heads[h][0]
        kj = k_ref[0, pl.ds(pl.multiple_of(j * MOBA_BLOCK, MOBA_BLOCK), MOBA_BLOCK),
                   p * PAIR_DIM:(p + 1) * PAIR_DIM]
        return jnp.dot(kj, q_heads[h], preferred_element_type=F32)

    n_heads = len(heads)
    init = []
    for h in range(n_heads):
        s = scores(0, h)
        s_scrs[h][0] = s
        init += [jnp.full((1, MOBA_BLOCK), NEG_INF, F32), jnp.max(s, axis=0, keepdims=True),
                 jnp.zeros((HEAD_DIM + ONES_ROWS, MOBA_BLOCK), F32)]

    def one_block(j, slot, carry):
        j_next = jnp.minimum(j + 1, i)
        out = []
        for h in range(n_heads):
            m, cmax, acc = carry[3 * h:3 * h + 3]
            brow = bias_scr[h, pl.ds(j, 1), :]
            m_new = jnp.maximum(m, cmax + brow)
            p = jnp.exp2(s_scrs[h][slot] + (brow - m_new)).astype(BF16)
            acc = jnp.exp2(m - m_new) * acc + jnp.dot(values_ext(j, h), p, preferred_element_type=F32)
            s_next = scores(j_next, h)
            s_scrs[h][1 - slot] = s_next
            out += [m_new, jnp.max(s_next, axis=0, keepdims=True), acc]
        return tuple(out)

    def body(t, carry):
        for u in range(ATTN_UNROLL):
            carry = one_block(ATTN_UNROLL * t + u, u & 1, carry)
        return carry

    n_iter = lax.shift_right_logical(i + (ATTN_UNROLL - 1), ATTN_UNROLL.bit_length() - 1)
    fin = lax.fori_loop(0, n_iter, body, tuple(init))

    key_pos = lax.broadcasted_iota(jnp.int32, (MOBA_BLOCK, MOBA_BLOCK), 0)
    q_pos = lax.broadcasted_iota(jnp.int32, (MOBA_BLOCK, MOBA_BLOCK), 1)
    causal_bias = jnp.where(key_pos <= q_pos, 0.0, MASK_BIAS)
    outs = []
    for h in range(n_heads):
        m, _, acc = fin[3 * h:3 * h + 3]
        s = s_scrs[h][i & 1] + causal_bias
        m_new = jnp.maximum(m, jnp.max(s, axis=0, keepdims=True))
        p = jnp.exp2(s - m_new).astype(BF16)
        acc = jnp.exp2(m - m_new) * acc + jnp.dot(values_ext(i, h), p, preferred_element_type=F32)
        outs.append(acc[:HEAD_DIM] / acc[HEAD_DIM:HEAD_DIM + 1])
    for p in range(ATTN_PAIRS):
        o_pair = jnp.concatenate(outs[2 * p:2 * p + 2], axis=0)
        y_ref[0, :, p * PAIR_DIM:(p + 1) * PAIR_DIM] = o_pair.T


def _attn(qt, k, vt):
    bsz, _, nb, _, _ = qt.shape
    assert nb % ATTN_UNROLL == 0 and ATTN_UNROLL % 2 == 0
    seq = nb * MOBA_BLOCK
    width = ATTN_PAIRS * PAIR_DIM
    n_heads = 2 * ATTN_PAIRS
    return pl.pallas_call(
        functools.partial(_attn_body, nb=nb),
        out_shape=jax.ShapeDtypeStruct((bsz, seq, ATTN_WIDTH), F32),
        grid=(bsz, HEAD_PAIRS // ATTN_PAIRS, nb),
        in_specs=[pl.BlockSpec((1, ATTN_PAIRS, 1, PAIR_DIM, MOBA_BLOCK), lambda b, p, i: (b, p, i, 0, 0)),
                  pl.BlockSpec((1, seq, width), lambda b, p, i: (b, 0, p)),
                  pl.BlockSpec((1, ATTN_PAIRS, nb, PAIR_DIM, MOBA_BLOCK),
                               lambda b, p, i: (b, p, 0, 0, 0))],
        out_specs=pl.BlockSpec((1, MOBA_BLOCK, width), lambda b, p, i: (b, i, p)),
        scratch_shapes=[pltpu.VMEM((nb, width), F32),
                        pltpu.VMEM((n_heads, nb, MOBA_BLOCK), F32)]
        + [pltpu.VMEM((2, MOBA_BLOCK, MOBA_BLOCK), F32) for _ in range(n_heads)],
        compiler_params=pltpu.CompilerParams(
            dimension_semantics=("arbitrary", "arbitrary", "arbitrary"),
            vmem_limit_bytes=VMEM_LIMIT),
        name="moba_attn",
    )(qt, k, vt)


def _mix_body(r_ref, *rest):
    st_scr, uprev_scr = rest[-2:]
    t = pl.program_id(0)

    @pl.when(t == 0)
    def _():
        st_scr[...] = jnp.zeros_like(st_scr)
        uprev_scr[...] = jnp.zeros_like(uprev_scr)

    for b in range(r_ref.shape[0]):
        _mix_one(b, t, r_ref, *rest)


def _mix_one(b, t, r_ref, lb_ref, onorm_ref, poolw_ref, pscale_ref, tri_ref, hones_ref, hmask_ref,
             y_ref, st_scr, uprev_scr):
    rows, w = MIX_ROWS, HGRN_WIDTH
    u = r_ref[b, :, 0:POOL_WIDTH]
    qh = r_ref[b, :, POOL_WIDTH:POOL_WIDTH + w]
    z = r_ref[b, :, POOL_WIDTH + w:POOL_WIDTH + 2 * w]
    value = r_ref[b, :, POOL_WIDTH + 2 * w:POOL_WIDTH + 3 * w]
    og = r_ref[b, :, POOL_WIDTH + 3 * w:POOL_WIDTH + 4 * w]

    lane = lax.broadcasted_iota(jnp.int32, (rows, w), 1)
    row = lax.broadcasted_iota(jnp.int32, (rows, w), 0)

    uext = jnp.concatenate([uprev_scr[b], u], axis=0)
    sums = []
    acc = uext
    for shift in (1, 2, 4, 8):
        acc = acc + pltpu.roll(acc, shift, 0)
        sums.append(acc[POOL_HALO:])
    in_g0, in_g01, in_g012 = (lane < g * POOL_GROUP_DIM for g in (1, 2, 3))
    win_sum = jnp.where(in_g0, sums[0], jnp.where(in_g01, sums[1], jnp.where(in_g012, sums[2], sums[3])))
    window = jnp.where(in_g0, POOL_WINDOWS[0],
                       jnp.where(in_g01, POOL_WINDOWS[1],
                                 jnp.where(in_g012, POOL_WINDOWS[2], POOL_WINDOWS[3])))
    count = jnp.minimum(t * rows + row + 1, window).astype(F32)
    diff = win_sum / count - u
    y_pool = jnp.dot(diff.astype(BF16), poolw_ref[...], preferred_element_type=F32) * pscale_ref[...]
    uprev_scr[b] = u[rows - POOL_HALO:]

    lb = lb_ref[...]
    log_sig = jnp.minimum(z, 0.0) - jnp.log1p(jnp.exp(-jnp.abs(z)))
    a_term = jnp.log(jnp.maximum(lb, LB_FLOOR))
    b_term = jnp.log1p(-lb) + log_sig
    log_f = jnp.maximum(a_term, b_term) + jnp.log1p(jnp.exp(-jnp.abs(a_term - b_term)))
    key = (1.0 - lb) * jax.nn.sigmoid(-z)
    query = _silu(qh) * (HGRN_HEAD_DIM ** -0.5)

    tri = tri_ref[...]
    f_hi = log_f.astype(BF16)
    f_rest = log_f - f_hi.astype(F32)
    f_mid = f_rest.astype(BF16)
    f_lo = (f_rest - f_mid.astype(F32)).astype(BF16)
    cum = (jnp.dot(tri, f_hi, preferred_element_type=F32) + jnp.dot(tri, f_mid, preferred_element_type=F32)
           + jnp.dot(tri, f_lo, preferred_element_type=F32)) * LOG2_E
    n_sub = rows // SUB
    last = [cum[(a + 1) * SUB - 1:(a + 1) * SUB] for a in range(n_sub)]
    tot = jnp.concatenate([jnp.broadcast_to(r, (SUB, w)) for r in last], axis=0)
    k_dec = (key * jnp.exp2(tot - cum)).astype(BF16)
    q_dec = (query * jnp.exp2(cum)).astype(BF16)
    value_b = value.astype(BF16)

    head_ones = hones_ref[...]
    head_mask = hmask_ref[...]
    sub_row = lax.broadcasted_iota(jnp.int32, (SUB, w), 0)
    half_row = lax.broadcasted_iota(jnp.int32, (SUB // 2, w), 0) + SUB // 2

    st = st_scr[b]
    outs = []
    for a in range(n_sub):
        rs = slice(a * SUB, (a + 1) * SUB)
        o_inter = lax.dot_general(q_dec[rs], st.astype(BF16), (((1,), (1,)), ((), ())),
                                  preferred_element_type=F32)
        cum_a, q_a, k_a, v_a = cum[rs], query[rs], key[rs], value[rs]
        pieces = []
        for s in range(SUB):
            if s < SUB // 2:
                decay = jnp.exp2(jnp.minimum(cum_a - cum_a[s:s + 1], 0.0))
                pieces.append(jnp.where(sub_row >= s, decay * (q_a * k_a[s:s + 1]), 0.0))
            else:
                decay = jnp.exp2(jnp.minimum(cum_a[SUB // 2:] - cum_a[s:s + 1], 0.0))
                pieces.append(jnp.where(half_row >= s, decay * (q_a[SUB // 2:] * k_a[s:s + 1]), 0.0))
        e_all = jnp.concatenate(pieces, axis=0).astype(BF16)
        scores = jnp.dot(e_all, head_ones, preferred_element_type=F32)
        o_top = jnp.zeros((SUB // 2, w), F32)
        o_bot = jnp.zeros((SUB // 2, w), F32)
        for s in range(SUB):
            if s < SUB // 2:
                o_top = o_top + scores[s * SUB:s * SUB + SUB // 2] * v_a[s:s + 1]
                o_bot = o_bot + scores[s * SUB + SUB // 2:(s + 1) * SUB] * v_a[s:s + 1]
            else:
                base = (SUB // 2) * SUB + (s - SUB // 2) * (SUB // 2)
                o_bot = o_bot + scores[base:base + SUB // 2] * v_a[s:s + 1]
        outs.append(o_inter + jnp.concatenate([o_top, o_bot], axis=0))
        upd_t = lax.dot_general(value_b[rs], k_dec[rs], (((0,), (0,)), ((), ())),
                                preferred_element_type=F32)
        st = jnp.exp2(last[a]) * st + upd_t * head_mask
    st_scr[b] = st

    o = jnp.concatenate(outs, axis=0)
    sq = o * o
    sq_hi = sq.astype(BF16)
    sq_lo = (sq - sq_hi.astype(F32)).astype(BF16)
    ms = (jnp.dot(sq_hi, head_ones, preferred_element_type=F32)
          + jnp.dot(sq_lo, head_ones, preferred_element_type=F32)) * (1.0 / HGRN_HEAD_DIM)
    y_hgrn = o * lax.rsqrt(ms + EPS) * onorm_ref[...] * _silu(og)
    y_ref[b, :, 0:POOL_WIDTH] = y_pool
    y_ref[b, :, POOL_WIDTH:] = y_hgrn


def _mix_constants():
    r = np.arange(MIX_ROWS)
    tri = ((r[:, None] // SUB == r[None, :] // SUB) & (r[None, :] <= r[:, None])).astype(np.float32)
    c = np.arange(HGRN_WIDTH) // HGRN_HEAD_DIM
    same_head = (c[:, None] == c[None, :]).astype(np.float32)
    return jnp.asarray(tri, BF16), jnp.asarray(same_head, BF16), jnp.asarray(same_head, F32)


def _mix(rest, lb_row, onorm_row, poolw_bd, pscale_row):
    bsz, seq, _ = rest.shape
    tri, head_ones, head_mask = _mix_constants()
    return pl.pallas_call(
        _mix_body,
        out_shape=jax.ShapeDtypeStruct((bsz, seq, POOL_WIDTH + HGRN_WIDTH), F32),
        grid=(seq // MIX_ROWS,),
        in_specs=[pl.BlockSpec((bsz, MIX_ROWS, REST_WIDTH), lambda t: (0, t, 0)),
                  _const_spec((1, HGRN_WIDTH)), _const_spec((1, HGRN_WIDTH)),
                  _const_spec((POOL_WIDTH, POOL_WIDTH)), _const_spec((1, POOL_WIDTH)),
                  _const_spec((MIX_ROWS, MIX_ROWS)), _const_spec((HGRN_WIDTH, HGRN_WIDTH)),
                  _const_spec((HGRN_WIDTH, HGRN_WIDTH))],
        out_specs=pl.BlockSpec((bsz, MIX_ROWS, POOL_WIDTH + HGRN_WIDTH), lambda t: (0, t, 0)),
        scratch_shapes=[pltpu.VMEM((bsz, HGRN_WIDTH, HGRN_WIDTH), F32),
                        pltpu.VMEM((bsz, POOL_HALO, POOL_WIDTH), F32)],
        compiler_params=pltpu.CompilerParams(dimension_semantics=("arbitrary",),
                                             vmem_limit_bytes=VMEM_LIMIT),
        name="pool_hgrn",
    )(rest, lb_row, onorm_row, poolw_bd, pscale_row, tri, head_ones, head_mask)


def _block_diag(blocks):
    g, c, e = blocks.shape
    out = jnp.zeros((g * c, g * e), blocks.dtype)
    for i in range(g):
        out = out.at[i * c:(i + 1) * c, i * e:(i + 1) * e].set(blocks[i])
    return out


def kernel(x, ffn1_norm, ffn1_w_gate, ffn1_w_up, ffn1_w_down, mix_norm, w_in, q_norm, k_norm,
           pool_w, pool_scale, hgrn_lb, hgrn_out_norm, w_out, ffn2_norm, ffn2_w_gate, ffn2_w_up,
           ffn2_w_down):
    bsz, seq, d = x.shape
    n = bsz * seq

    inv_freq = ROPE_THETA ** (-jnp.arange(HALF, dtype=F32) / HALF)
    ang = jnp.arange(seq).astype(F32)[:, None] * inv_freq[None, :]
    cos_t, sin_t = jnp.cos(ang).T, jnp.sin(ang).T

    lb_soft = jax.nn.softmax(hgrn_lb.astype(F32), axis=0)
    lower_bounds = jnp.concatenate([jnp.zeros_like(lb_soft[:1]), jnp.cumsum(lb_soft[:-1], axis=0)],
                                   axis=0)

    ffn1 = (ffn1_norm[:, None, :], ffn1_w_gate.astype(BF16), ffn1_w_up.astype(BF16),
            ffn1_w_down.astype(BF16))
    ffn2 = (ffn2_norm[:, None, :], ffn2_w_gate.astype(BF16), ffn2_w_up.astype(BF16),
            ffn2_w_down.astype(BF16))
    w_in_b = w_in.astype(BF16)
    w_qkv_t = jnp.swapaxes(w_in_b[:, :, :3 * ATTN_WIDTH], 1, 2)
    w_rest = w_in_b[:, :, 3 * ATTN_WIDTH:]
    w_out_b = w_out.astype(BF16)

    x2d = x.reshape(n, d)
    for l in range(DEPTH):
        x2d = _ffn(x2d, l, *ffn1)
        qt, k, vt, rest = _proj(x2d.reshape(bsz, seq, d), mix_norm[l][None, :], w_qkv_t[l], w_rest[l],
                                q_norm[l][:, None], k_norm[l][:, None], cos_t, sin_t)
        y_attn = _attn(qt, k, vt)
        y_rest = _mix(rest, lower_bounds[l][None, :], jnp.tile(hgrn_out_norm[l], 4)[None, :],
                      _block_diag(pool_w[l]).astype(BF16), pool_scale[l][None, :])
        x2d = _outproj_ffn(x2d, y_attn.reshape(n, ATTN_WIDTH), y_rest.reshape(n, ATTN_WIDTH),
                           l, w_out_b, *ffn2)
    return x2d.reshape(bsz, seq, d)
```

```python
import functools

import jax
import jax.numpy as jnp
import numpy as np
from jax import lax
from jax.experimental import pallas as pl
from jax.experimental.pallas import tpu as pltpu

F32 = jnp.float32
BF16 = jnp.bfloat16

D_MODEL = 1024
DEPTH = 4
ATTN_HEADS = 8
HEAD_DIM = 64
HALF = HEAD_DIM // 2
ATTN_WIDTH = ATTN_HEADS * HEAD_DIM
HEAD_PAIRS = ATTN_HEADS // 2
PAIR_DIM = 2 * HEAD_DIM
POOL_WINDOWS = (2, 4, 8, 16)
POOL_GROUP_DIM = 64
POOL_WIDTH = 256
HGRN_HEAD_DIM = 64
HGRN_WIDTH = 256
REST_WIDTH = POOL_WIDTH + 4 * HGRN_WIDTH
D_FF = 2816
FF_CHUNK = 256
N_FF_CHUNKS = D_FF // FF_CHUNK
MOBA_BLOCK = 256
MOBA_TOPK = 3
ROPE_THETA = 10000.0
EPS = 1e-6
NEG_INF = -1e30
LB_FLOOR = 1e-20

FFN_ROWS = 1024
CAST_SLABS = 16
PROJ_ROWS = 512
MIX_ROWS = 128
SUB = 16
POOL_HALO = 16
ATTN_PAIRS = 4
ATTN_UNROLL = 4
ONES_ROWS = 16
MASK_BIAS = 2 * NEG_INF
LOG2_E = 1.4426950408889634
VMEM_LIMIT = 56 * 1024 * 1024


def _rms_rows(x, gain_row):
    ms = jnp.mean(x * x, axis=-1, keepdims=True)
    return x * lax.rsqrt(ms + EPS) * gain_row


def _silu(x):
    return x * jax.nn.sigmoid(x)


def _swiglu_half_step(x, gain_row, wg_ref, wu_ref, wd_ref):
    h = _rms_rows(x, gain_row).astype(BF16)
    acc = jnp.zeros_like(x)
    for c in range(N_FF_CHUNKS):
        cols = slice(c * FF_CHUNK, (c + 1) * FF_CHUNK)
        g = jnp.dot(h, wg_ref[:, cols], preferred_element_type=F32)
        u = jnp.dot(h, wu_ref[:, cols], preferred_element_type=F32)
        acc = acc + jnp.dot((_silu(g) * u).astype(BF16), wd_ref[cols, :],
                            preferred_element_type=F32)
    return x + 0.5 * acc


def _ffn_body(*refs, n_lead, n_cast):
    x = refs[0][...]
    if n_lead:
        ya_ref, yr_ref, wo_ref = refs[1:4]
        x = x + jnp.dot(ya_ref[...], wo_ref[0, :ATTN_WIDTH, :], preferred_element_type=F32)
        x = x + jnp.dot(yr_ref[...], wo_ref[0, ATTN_WIDTH:, :], preferred_element_type=F32)
    gain_ref, wg_ref, wu_ref, wd_ref = refs[1 + n_lead:5 + n_lead]
    src = refs[5 + n_lead:5 + n_lead + n_cast]
    o_ref = refs[5 + n_lead + n_cast]
    dst = refs[6 + n_lead + n_cast:]
    o_ref[...] = _swiglu_half_step(x, gain_ref[0], wg_ref, wu_ref, wd_ref)
    for s_ref, d_ref in zip(src, dst):
        d_ref[...] = s_ref[0].astype(BF16)


def _const_spec(shape):
    return pl.BlockSpec(shape, lambda *_: (0,) * len(shape), pipeline_mode=pl.Buffered(1))


def _layer_spec(shape, layer):
    return pl.BlockSpec((1,) + shape, lambda *_: (layer,) + (0,) * len(shape),
                        pipeline_mode=pl.Buffered(1))


def _ffn(x2d, gains, layer, weights, lead=(), w_out=None, cast_from=(), cast_layer=0):
    n = x2d.shape[0]
    tile = FFN_ROWS
    steps = n // tile
    per_slab = max(steps // CAST_SLABS, 1)
    row_spec = pl.BlockSpec((tile, D_MODEL), lambda i: (i, 0))
    half_spec = pl.BlockSpec((tile, ATTN_WIDTH), lambda i: (i, 0))
    args, in_specs = [x2d], [row_spec]
    if lead:
        args += [*lead, w_out]
        in_specs += [half_spec, half_spec, _layer_spec((D_MODEL, D_MODEL), layer)]
    args += [gains, *weights]
    in_specs += [_layer_spec((1, D_MODEL), layer), _const_spec((D_MODEL, D_FF)),
                 _const_spec((D_MODEL, D_FF)), _const_spec((D_FF, D_MODEL))]
    out_shape = [jax.ShapeDtypeStruct((n, D_MODEL), F32)]
    out_specs = [row_spec]
    for w in cast_from:
        rows, cols = w.shape[1] * per_slab // steps, w.shape[2]
        args.append(w)
        in_specs.append(pl.BlockSpec((1, rows, cols), lambda i: (cast_layer, i // per_slab, 0)))
        out_shape.append(jax.ShapeDtypeStruct(w.shape[1:], BF16))
        out_specs.append(pl.BlockSpec((rows, cols), lambda i: (i // per_slab, 0)))
    outs = pl.pallas_call(
        functools.partial(_ffn_body, n_lead=3 if lead else 0, n_cast=len(cast_from)),
        out_shape=out_shape,
        grid=(steps,),
        in_specs=in_specs,
        out_specs=out_specs,
        compiler_params=pltpu.CompilerParams(dimension_semantics=("arbitrary",),
                                             vmem_limit_bytes=VMEM_LIMIT),
        name="outproj_ffn" if lead else "ffn",
    )(*args)
    return outs[0], tuple(outs[1:])


def _proj_body(x_ref, gain_ref, wt_ref, wr_ref, qn_ref, kn_ref, cos_ref, sin_ref,
               qt_ref, k_ref, vt_ref, rest_ref):
    h = _rms_rows(x_ref[0], gain_ref[...]).astype(BF16)
    rest_ref[0] = jnp.dot(h, wr_ref[...], preferred_element_type=F32)
    qkvt = lax.dot_general(wt_ref[...], h, (((1,), (1,)), ((), ())), preferred_element_type=F32)
    cos = cos_ref[...]
    sin = sin_ref[...]

    def norm_rope(t, gain_col):
        ms = jnp.mean(t * t, axis=0, keepdims=True)
        tn = t * lax.rsqrt(ms + EPS) * gain_col
        x1, x2 = tn[:HALF], tn[HALF:]
        return jnp.concatenate([x1 * cos - x2 * sin, x2 * cos + x1 * sin], axis=0)

    n_blk = PROJ_ROWS // MOBA_BLOCK
    for p in range(HEAD_PAIRS):
        lo = p * PAIR_DIM
        q_pair = jnp.concatenate(
            [norm_rope(qkvt[lo + e * HEAD_DIM:lo + (e + 1) * HEAD_DIM], qn_ref[...]) for e in range(2)],
            axis=0)
        k_pair = jnp.concatenate(
            [norm_rope(qkvt[ATTN_WIDTH + lo + e * HEAD_DIM:ATTN_WIDTH + lo + (e + 1) * HEAD_DIM],
                       kn_ref[...]) for e in range(2)], axis=0)
        v_pair = qkvt[2 * ATTN_WIDTH + lo:2 * ATTN_WIDTH + lo + PAIR_DIM].astype(BF16)
        k_ref[0, :, lo:lo + PAIR_DIM] = k_pair.T.astype(BF16)
        for b in range(n_blk):
            qt_ref[0, p, b] = q_pair[:, b * MOBA_BLOCK:(b + 1) * MOBA_BLOCK]
            vt_ref[0, p, b] = v_pair[:, b * MOBA_BLOCK:(b + 1) * MOBA_BLOCK]


def _proj(x3d, gain_row, wt, wr, qn_col, kn_col, cos_t, sin_t):
    bsz, seq, _ = x3d.shape
    nb = seq // MOBA_BLOCK
    n_blk = PROJ_ROWS // MOBA_BLOCK
    blk_spec = pl.BlockSpec((1, HEAD_PAIRS, n_blk, PAIR_DIM, MOBA_BLOCK), lambda b, i: (b, 0, i, 0, 0))
    return pl.pallas_call(
        _proj_body,
        out_shape=(jax.ShapeDtypeStruct((bsz, HEAD_PAIRS, nb, PAIR_DIM, MOBA_BLOCK), F32),
                   jax.ShapeDtypeStruct((bsz, seq, ATTN_WIDTH), BF16),
                   jax.ShapeDtypeStruct((bsz, HEAD_PAIRS, nb, PAIR_DIM, MOBA_BLOCK), BF16),
                   jax.ShapeDtypeStruct((bsz, seq, REST_WIDTH), F32)),
        grid=(bsz, seq // PROJ_ROWS),
        in_specs=[pl.BlockSpec((1, PROJ_ROWS, D_MODEL), lambda b, i: (b, i, 0)),
                  _const_spec((1, D_MODEL)),
                  _const_spec((3 * ATTN_WIDTH, D_MODEL)),
                  _const_spec((D_MODEL, REST_WIDTH)),
                  _const_spec((HEAD_DIM, 1)), _const_spec((HEAD_DIM, 1)),
                  pl.BlockSpec((HALF, PROJ_ROWS), lambda b, i: (0, i)),
                  pl.BlockSpec((HALF, PROJ_ROWS), lambda b, i: (0, i))],
        out_specs=(blk_spec,
                   pl.BlockSpec((1, PROJ_ROWS, ATTN_WIDTH), lambda b, i: (b, i, 0)),
                   blk_spec,
                   pl.BlockSpec((1, PROJ_ROWS, REST_WIDTH), lambda b, i: (b, i, 0))),
        compiler_params=pltpu.CompilerParams(dimension_semantics=("arbitrary", "arbitrary"),
                                             vmem_limit_bytes=VMEM_LIMIT),
        name="proj",
    )(x3d, gain_row, wt, wr, qn_col, kn_col, cos_t, sin_t)


def _attn_body(qt_ref, k_ref, vt_ref, y_ref, kmean_scr, bias_scr, *s_scrs, nb):
    i = pl.program_id(2)
    width = ATTN_PAIRS * PAIR_DIM

    @pl.when(i == 0)
    def _():
        for j in range(nb):
            kb = k_ref[0, j * MOBA_BLOCK:(j + 1) * MOBA_BLOCK, :].astype(F32)
            kmean_scr[j:j + 1, :] = jnp.sum(kb, axis=0, keepdims=True) * (1.0 / MOBA_BLOCK)

    kmean = kmean_scr[...]
    lane_p = lax.broadcasted_iota(jnp.int32, (nb, PAIR_DIM), 1)
    qrow = lax.broadcasted_iota(jnp.int32, (PAIR_DIM, MOBA_BLOCK), 0)
    blk = lax.broadcasted_iota(jnp.int32, (nb, MOBA_BLOCK), 0)
    valid = blk < i

    heads = [(p, e) for p in range(ATTN_PAIRS) for e in range(2)]

    def split(v):
        hi = v.astype(BF16)
        return hi, (v - hi.astype(F32)).astype(BF16)

    gates = []
    for p in range(ATTN_PAIRS):
        q_hi, q_lo = split(qt_ref[0, p, 0])
        km_pair = kmean[:, p * PAIR_DIM:(p + 1) * PAIR_DIM]
        km_hi, km_lo = split(jnp.concatenate(
            [jnp.where((lane_p >= e * HEAD_DIM) & (lane_p < (e + 1) * HEAD_DIM), km_pair, 0.0)
             for e in range(2)], axis=0))
        gates.append(jnp.dot(km_hi, q_hi, preferred_element_type=F32)
                     + jnp.dot(km_lo, q_hi, preferred_element_type=F32)
                     + jnp.dot(km_hi, q_lo, preferred_element_type=F32))

    q_heads = []
    for h, (p, e) in enumerate(heads):
        qt = qt_ref[0, p, 0]
        gate = gates[p][e * nb:(e + 1) * nb]
        sel = jnp.zeros((nb, MOBA_BLOCK), F32)
        for _ in range(MOBA_TOPK):
            cand = jnp.where(valid & (sel == 0.0), gate, -jnp.inf)
            best = jnp.max(cand, axis=0, keepdims=True)
            is_best = (cand == best) & (best > -jnp.inf)
            first = jnp.min(jnp.where(is_best, blk, nb), axis=0, keepdims=True)
            sel = jnp.where(blk == first, 1.0, sel)
        bias_scr[h] = jnp.where(sel > 0.0, 0.0, MASK_BIAS)
        in_rows = (qrow >= e * HEAD_DIM) & (qrow < (e + 1) * HEAD_DIM)
        q_heads.append(jnp.where(in_rows, qt * (HEAD_DIM ** -0.5 * LOG2_E), 0.0).astype(BF16))

    ones_rows = jnp.ones((ONES_ROWS, MOBA_BLOCK), BF16)

    def values_ext(j, h):
        p, e = heads[h]
        return jnp.concatenate([vt_ref[0, p, j][e * HEAD_DIM:(e + 1) * HEAD_DIM, :], ones_rows], axis=0)

    def scores(j, h):
        p = heads[h][0]
        kj = k_ref[0, pl.ds(pl.multiple_of(j * MOBA_BLOCK, MOBA_BLOCK), MOBA_BLOCK),
                   p * PAIR_DIM:(p + 1) * PAIR_DIM]
        return jnp.dot(kj, q_heads[h], preferred_element_type=F32)

    n_heads = len(heads)
    init = []
    for h in range(n_heads):
        s = scores(0, h)
        s_scrs[h][0] = s
        init += [jnp.full((1, MOBA_BLOCK), NEG_INF, F32), jnp.max(s, axis=0, keepdims=True),
                 jnp.zeros((HEAD_DIM + ONES_ROWS, MOBA_BLOCK), F32)]

    def one_block(j, slot, carry):
        out = []
        for h in range(n_heads):
            m, cmax, acc = carry[3 * h:3 * h + 3]
            brow = bias_scr[h, pl.ds(j, 1), :]
            m_new = jnp.maximum(m, cmax + brow)
            p = jnp.exp2(s_scrs[h][slot] + (brow - m_new)).astype(BF16)
            acc = jnp.exp2(m - m_new) * acc + jnp.dot(values_ext(j, h), p, preferred_element_type=F32)
            s_next = scores(j + 1, h)
            s_scrs[h][1 - slot] = s_next
            out += [m_new, jnp.max(s_next, axis=0, keepdims=True), acc]
        return tuple(out)

    def body(t, carry):
        for u in range(ATTN_UNROLL):
            carry = one_block(ATTN_UNROLL * t + u, u & 1, carry)
        return carry

    n_unrolled = lax.shift_right_logical(i, ATTN_UNROLL.bit_length() - 1)
    mid = lax.fori_loop(0, n_unrolled, body, tuple(init))
    fin = lax.fori_loop(n_unrolled * ATTN_UNROLL, i, lambda j, c: one_block(j, j & 1, c), mid)

    key_pos = lax.broadcasted_iota(jnp.int32, (MOBA_BLOCK, MOBA_BLOCK), 0)
    q_pos = lax.broadcasted_iota(jnp.int32, (MOBA_BLOCK, MOBA_BLOCK), 1)
    causal_bias = jnp.where(key_pos <= q_pos, 0.0, MASK_BIAS)
    outs = []
    for h in range(n_heads):
        m, _, acc = fin[3 * h:3 * h + 3]
        s = s_scrs[h][i & 1] + causal_bias
        m_new = jnp.maximum(m, jnp.max(s, axis=0, keepdims=True))
        p = jnp.exp2(s - m_new).astype(BF16)
        acc = jnp.exp2(m - m_new) * acc + jnp.dot(values_ext(i, h), p, preferred_element_type=F32)
        outs.append(acc[:HEAD_DIM] / acc[HEAD_DIM:HEAD_DIM + 1])
    for p in range(ATTN_PAIRS):
        o_pair = jnp.concatenate(outs[2 * p:2 * p + 2], axis=0)
        y_ref[0, :, p * PAIR_DIM:(p + 1) * PAIR_DIM] = o_pair.T.astype(BF16)


def _attn(qt, k, vt):
    bsz, _, nb, _, _ = qt.shape
    assert ATTN_UNROLL % 2 == 0
    seq = nb * MOBA_BLOCK
    width = ATTN_PAIRS * PAIR_DIM
    n_heads = 2 * ATTN_PAIRS
    return pl.pallas_call(
        functools.partial(_attn_body, nb=nb),
        out_shape=jax.ShapeDtypeStruct((bsz, seq, ATTN_WIDTH), BF16),
        grid=(bsz, HEAD_PAIRS // ATTN_PAIRS, nb),
        in_specs=[pl.BlockSpec((1, ATTN_PAIRS, 1, PAIR_DIM, MOBA_BLOCK), lambda b, p, i: (b, p, i, 0, 0)),
                  pl.BlockSpec((1, seq, width), lambda b, p, i: (b, 0, p)),
                  pl.BlockSpec((1, ATTN_PAIRS, nb, PAIR_DIM, MOBA_BLOCK),
                               lambda b, p, i: (b, p, 0, 0, 0))],
        out_specs=pl.BlockSpec((1, MOBA_BLOCK, width), lambda b, p, i: (b, i, p)),
        scratch_shapes=[pltpu.VMEM((nb, width), F32),
                        pltpu.VMEM((n_heads, nb, MOBA_BLOCK), F32)]
        + [pltpu.VMEM((2, MOBA_BLOCK, MOBA_BLOCK), F32) for _ in range(n_heads)],
        compiler_params=pltpu.CompilerParams(
            dimension_semantics=("arbitrary", "arbitrary", "arbitrary"),
            vmem_limit_bytes=VMEM_LIMIT),
        name="moba_attn",
    )(qt, k, vt)


def _mix_body(r_ref, *rest):
    st_scr, uprev_scr = rest[-2:]
    t = pl.program_id(0)

    @pl.when(t == 0)
    def _():
        st_scr[...] = jnp.zeros_like(st_scr)
        uprev_scr[...] = jnp.zeros_like(uprev_scr)

    for b in range(r_ref.shape[0]):
        _mix_one(b, t, r_ref, *rest)


def _mix_one(b, t, r_ref, lb_ref, onorm_ref, poolw_ref, pscale_ref, tri_ref, hones_ref, hmask_ref,
             y_ref, st_scr, uprev_scr):
    rows, w = MIX_ROWS, HGRN_WIDTH
    u = r_ref[b, :, 0:POOL_WIDTH]
    qh = r_ref[b, :, POOL_WIDTH:POOL_WIDTH + w]
    z = r_ref[b, :, POOL_WIDTH + w:POOL_WIDTH + 2 * w]
    value = r_ref[b, :, POOL_WIDTH + 2 * w:POOL_WIDTH + 3 * w]
    og = r_ref[b, :, POOL_WIDTH + 3 * w:POOL_WIDTH + 4 * w]

    lane = lax.broadcasted_iota(jnp.int32, (rows, w), 1)
    row = lax.broadcasted_iota(jnp.int32, (rows, w), 0)

    uext = jnp.concatenate([uprev_scr[b], u], axis=0)
    sums = []
    acc = uext
    for shift in (1, 2, 4, 8):
        acc = acc + pltpu.roll(acc, shift, 0)
        sums.append(acc[POOL_HALO:])
    in_g0, in_g01, in_g012 = (lane < g * POOL_GROUP_DIM for g in (1, 2, 3))
    win_sum = jnp.where(in_g0, sums[0], jnp.where(in_g01, sums[1], jnp.where(in_g012, sums[2], sums[3])))
    window = jnp.where(in_g0, POOL_WINDOWS[0],
                       jnp.where(in_g01, POOL_WINDOWS[1],
                                 jnp.where(in_g012, POOL_WINDOWS[2], POOL_WINDOWS[3])))
    count = jnp.minimum(t * rows + row + 1, window).astype(F32)
    diff = win_sum / count - u
    y_pool = jnp.dot(diff.astype(BF16), poolw_ref[...], preferred_element_type=F32) * pscale_ref[...]
    uprev_scr[b] = u[rows - POOL_HALO:]

    lb = lb_ref[...]
    log_sig = jnp.minimum(z, 0.0) - jnp.log1p(jnp.exp(-jnp.abs(z)))
    a_term = jnp.log(jnp.maximum(lb, LB_FLOOR))
    b_term = jnp.log1p(-lb) + log_sig
    log_f = jnp.maximum(a_term, b_term) + jnp.log1p(jnp.exp(-jnp.abs(a_term - b_term)))
    key = (1.0 - lb) * jax.nn.sigmoid(-z)
    query = _silu(qh) * (HGRN_HEAD_DIM ** -0.5)

    tri = tri_ref[...]
    f_hi = log_f.astype(BF16)
    f_rest = log_f - f_hi.astype(F32)
    f_mid = f_rest.astype(BF16)
    f_lo = (f_rest - f_mid.astype(F32)).astype(BF16)
    cum = (jnp.dot(tri, f_hi, preferred_element_type=F32) + jnp.dot(tri, f_mid, preferred_element_type=F32)
           + jnp.dot(tri, f_lo, preferred_element_type=F32)) * LOG2_E
    n_sub = rows // SUB
    last = [cum[(a + 1) * SUB - 1:(a + 1) * SUB] for a in range(n_sub)]
    tot = jnp.concatenate([jnp.broadcast_to(r, (SUB, w)) for r in last], axis=0)
    k_dec = (key * jnp.exp2(tot - cum)).astype(BF16)
    q_dec = (query * jnp.exp2(cum)).astype(BF16)
    value_b = value.astype(BF16)

    head_ones = hones_ref[...]
    head_mask = hmask_ref[...]
    sub_row = lax.broadcasted_iota(jnp.int32, (SUB, w), 0)
    half_row = lax.broadcasted_iota(jnp.int32, (SUB // 2, w), 0) + SUB // 2

    st = st_scr[b]
    outs = []
    for a in range(n_sub):
        rs = slice(a * SUB, (a + 1) * SUB)
        o_inter = lax.dot_general(q_dec[rs], st.astype(BF16), (((1,), (1,)), ((), ())),
                                  preferred_element_type=F32)
        cum_a, q_a, k_a, v_a = cum[rs], query[rs], key[rs], value[rs]
        pieces = []
        for s in range(SUB):
            if s < SUB // 2:
                decay = jnp.exp2(jnp.minimum(cum_a - cum_a[s:s + 1], 0.0))
                pieces.append(jnp.where(sub_row >= s, decay * (q_a * k_a[s:s + 1]), 0.0))
            else:
                decay = jnp.exp2(jnp.minimum(cum_a[SUB // 2:] - cum_a[s:s + 1], 0.0))
                pieces.append(jnp.where(half_row >= s, decay * (q_a[SUB // 2:] * k_a[s:s + 1]), 0.0))
        e_all = jnp.concatenate(pieces, axis=0).astype(BF16)
        scores = jnp.dot(e_all, head_ones, preferred_element_type=F32)
        o_top = jnp.zeros((SUB // 2, w), F32)
        o_bot = jnp.zeros((SUB // 2, w), F32)
        for s in range(SUB):
            if s < SUB // 2:
                o_top = o_top + scores[s * SUB:s * SUB + SUB // 2] * v_a[s:s + 1]
                o_bot = o_bot + scores[s * SUB + SUB // 2:(s + 1) * SUB] * v_a[s:s + 1]
            else:
                base = (SUB // 2) * SUB + (s - SUB // 2) * (SUB // 2)
                o_bot = o_bot + scores[base:base + SUB // 2] * v_a[s:s + 1]
        outs.append(o_inter + jnp.concatenate([o_top, o_bot], axis=0))
        upd_t = lax.dot_general(value_b[rs], k_dec[rs], (((0,), (0,)), ((), ())),
                                preferred_element_type=F32)
        st = jnp.exp2(last[a]) * st + upd_t * head_mask
    st_scr[b] = st

    o = jnp.concatenate(outs, axis=0)
    sq = o * o
    sq_hi = sq.astype(BF16)
    sq_lo = (sq - sq_hi.astype(F32)).astype(BF16)
    ms = (jnp.dot(sq_hi, head_ones, preferred_element_type=F32)
          + jnp.dot(sq_lo, head_ones, preferred_element_type=F32)) * (1.0 / HGRN_HEAD_DIM)
    y_hgrn = o * lax.rsqrt(ms + EPS) * onorm_ref[...] * _silu(og)
    y_ref[b, :, 0:POOL_WIDTH] = y_pool.astype(BF16)
    y_ref[b, :, POOL_WIDTH:] = y_hgrn.astype(BF16)


def _mix_constants():
    r = np.arange(MIX_ROWS)
    tri = ((r[:, None] // SUB == r[None, :] // SUB) & (r[None, :] <= r[:, None])).astype(np.float32)
    c = np.arange(HGRN_WIDTH) // HGRN_HEAD_DIM
    same_head = (c[:, None] == c[None, :]).astype(np.float32)
    return jnp.asarray(tri, BF16), jnp.asarray(same_head, BF16), jnp.asarray(same_head, F32)


def _mix(rest, lb_row, onorm_row, poolw_bd, pscale_row):
    bsz, seq, _ = rest.shape
    tri, head_ones, head_mask = _mix_constants()
    return pl.pallas_call(
        _mix_body,
        out_shape=jax.ShapeDtypeStruct((bsz, seq, POOL_WIDTH + HGRN_WIDTH), BF16),
        grid=(seq // MIX_ROWS,),
        in_specs=[pl.BlockSpec((bsz, MIX_ROWS, REST_WIDTH), lambda t: (0, t, 0)),
                  _const_spec((1, HGRN_WIDTH)), _const_spec((1, HGRN_WIDTH)),
                  _const_spec((POOL_WIDTH, POOL_WIDTH)), _const_spec((1, POOL_WIDTH)),
                  _const_spec((MIX_ROWS, MIX_ROWS)), _const_spec((HGRN_WIDTH, HGRN_WIDTH)),
                  _const_spec((HGRN_WIDTH, HGRN_WIDTH))],
        out_specs=pl.BlockSpec((bsz, MIX_ROWS, POOL_WIDTH + HGRN_WIDTH), lambda t: (0, t, 0)),
        scratch_shapes=[pltpu.VMEM((bsz, HGRN_WIDTH, HGRN_WIDTH), F32),
                        pltpu.VMEM((bsz, POOL_HALO, POOL_WIDTH), F32)],
        compiler_params=pltpu.CompilerParams(dimension_semantics=("arbitrary",),
                                             vmem_limit_bytes=VMEM_LIMIT),
        name="pool_hgrn",
    )(rest, lb_row, onorm_row, poolw_bd, pscale_row, tri, head_ones, head_mask)


def _block_diag(blocks):
    g, c, e = blocks.shape
    out = jnp.zeros((g * c, g * e), blocks.dtype)
    for i in range(g):
        out = out.at[i * c:(i + 1) * c, i * e:(i + 1) * e].set(blocks[i])
    return out


def kernel(x, ffn1_norm, ffn1_w_gate, ffn1_w_up, ffn1_w_down, mix_norm, w_in, q_norm, k_norm,
           pool_w, pool_scale, hgrn_lb, hgrn_out_norm, w_out, ffn2_norm, ffn2_w_gate, ffn2_w_up,
           ffn2_w_down):
    bsz, seq, d = x.shape
    n = bsz * seq

    inv_freq = ROPE_THETA ** (-jnp.arange(HALF, dtype=F32) / HALF)
    ang = jnp.arange(seq).astype(F32)[:, None] * inv_freq[None, :]
    cos_t, sin_t = jnp.cos(ang).T, jnp.sin(ang).T

    lb_soft = jax.nn.softmax(hgrn_lb.astype(F32), axis=0)
    lower_bounds = jnp.concatenate([jnp.zeros_like(lb_soft[:1]), jnp.cumsum(lb_soft[:-1], axis=0)],
                                   axis=0)

    ffn1_f32 = (ffn1_w_gate, ffn1_w_up, ffn1_w_down)
    ffn2_f32 = (ffn2_w_gate, ffn2_w_up, ffn2_w_down)
    gains1, gains2 = ffn1_norm[:, None, :], ffn2_norm[:, None, :]
    w1 = tuple(w[0].astype(BF16) for w in ffn1_f32)
    w_in_b = w_in.astype(BF16)
    w_qkv_t = jnp.swapaxes(w_in_b[:, :, :3 * ATTN_WIDTH], 1, 2)
    w_rest = w_in_b[:, :, 3 * ATTN_WIDTH:]
    w_out_b = w_out.astype(BF16)

    x2d = x.reshape(n, d)
    for l in range(DEPTH):
        x2d, w2 = _ffn(x2d, gains1, l, w1, cast_from=ffn2_f32, cast_layer=l)
        qt, k, vt, rest = _proj(x2d.reshape(bsz, seq, d), mix_norm[l][None, :], w_qkv_t[l], w_rest[l],
                                q_norm[l][:, None], k_norm[l][:, None], cos_t, sin_t)
        y_attn = _attn(qt, k, vt)
        y_rest = _mix(rest, lower_bounds[l][None, :], jnp.tile(hgrn_out_norm[l], 4)[None, :],
                      _block_diag(pool_w[l]).astype(BF16), pool_scale[l][None, :])
        lead = (y_attn.reshape(n, ATTN_WIDTH), y_rest.reshape(n, ATTN_WIDTH))
        last = l == DEPTH - 1
        x2d, w1 = _ffn(x2d, gains2, l, w2, lead=lead, w_out=w_out_b,
                       cast_from=() if last else ffn1_f32, cast_layer=0 if last else l + 1)
    return x2d.reshape(bsz, seq, d)
```

```python
import functools

import jax
import jax.numpy as jnp
import numpy as np
from jax import lax
from jax.experimental import pallas as pl
from jax.experimental.pallas import tpu as pltpu

F32 = jnp.float32
BF16 = jnp.bfloat16

D_MODEL = 1024
DEPTH = 4
ATTN_HEADS = 8
HEAD_DIM = 64
HALF = HEAD_DIM // 2
ATTN_WIDTH = ATTN_HEADS * HEAD_DIM
HEAD_PAIRS = ATTN_HEADS // 2
PAIR_DIM = 2 * HEAD_DIM
POOL_WINDOWS = (2, 4, 8, 16)
POOL_GROUP_DIM = 64
POOL_WIDTH = 256
HGRN_HEAD_DIM = 64
HGRN_WIDTH = 256
REST_WIDTH = POOL_WIDTH + 4 * HGRN_WIDTH
D_FF = 2816
FF_CHUNK = 256
N_FF_CHUNKS = D_FF // FF_CHUNK
MOBA_BLOCK = 256
MOBA_TOPK = 3
ROPE_THETA = 10000.0
EPS = 1e-6
NEG_INF = -1e30
LB_FLOOR = 1e-20

FFN_ROWS = 1024
CAST_SLABS = 16
PROJ_ROWS = 512
MIX_ROWS = 128
MIX_FFN_ROWS = 512
SUB = 16
POOL_HALO = 16
ATTN_PAIRS = 4
ATTN_UNROLL = 4
ONES_ROWS = 16
MASK_BIAS = 2 * NEG_INF
LOG2_E = 1.4426950408889634
VMEM_LIMIT = 56 * 1024 * 1024


def _rms_rows(x, gain_row):
    ms = jnp.mean(x * x, axis=-1, keepdims=True)
    return x * lax.rsqrt(ms + EPS) * gain_row


def _silu(x):
    return x * jax.nn.sigmoid(x)


def _swiglu_half_step(x, gain_row, wg_ref, wu_ref, wd_ref, between=None):
    h = _rms_rows(x, gain_row).astype(BF16)
    acc = jnp.zeros_like(x)
    for c in range(N_FF_CHUNKS):
        cols = slice(c * FF_CHUNK, (c + 1) * FF_CHUNK)
        g = jnp.dot(h, wg_ref[:, cols], preferred_element_type=F32)
        u = jnp.dot(h, wu_ref[:, cols], preferred_element_type=F32)
        acc = acc + jnp.dot((_silu(g) * u).astype(BF16), wd_ref[cols, :],
                            preferred_element_type=F32)
        if between is not None:
            between(c)
    return x + 0.5 * acc


def _ffn_body(*refs, n_lead, n_cast):
    x = refs[0][...]
    if n_lead:
        ya_ref, yr_ref, wo_ref = refs[1:4]
        x = x + jnp.dot(ya_ref[...], wo_ref[0, :ATTN_WIDTH, :], preferred_element_type=F32)
        x = x + jnp.dot(yr_ref[...], wo_ref[0, ATTN_WIDTH:, :], preferred_element_type=F32)
    gain_ref, wg_ref, wu_ref, wd_ref = refs[1 + n_lead:5 + n_lead]
    src = refs[5 + n_lead:5 + n_lead + n_cast]
    o_ref = refs[5 + n_lead + n_cast]
    dst = refs[6 + n_lead + n_cast:]
    o_ref[...] = _swiglu_half_step(x, gain_ref[0], wg_ref, wu_ref, wd_ref)
    for s_ref, d_ref in zip(src, dst):
        d_ref[...] = s_ref[0].astype(BF16)


def _const_spec(shape):
    return pl.BlockSpec(shape, lambda *_: (0,) * len(shape), pipeline_mode=pl.Buffered(1))


def _layer_spec(shape, layer):
    return pl.BlockSpec((1,) + shape, lambda *_: (layer,) + (0,) * len(shape),
                        pipeline_mode=pl.Buffered(1))


def _ffn(x2d, gains, layer, weights, lead=(), w_out=None, cast_from=(), cast_layer=0):
    n = x2d.shape[0]
    tile = FFN_ROWS
    steps = n // tile
    per_slab = max(steps // CAST_SLABS, 1)
    row_spec = pl.BlockSpec((tile, D_MODEL), lambda i: (i, 0))
    half_spec = pl.BlockSpec((tile, ATTN_WIDTH), lambda i: (i, 0))
    args, in_specs = [x2d], [row_spec]
    if lead:
        args += [*lead, w_out]
        in_specs += [half_spec, half_spec, _layer_spec((D_MODEL, D_MODEL), layer)]
    args += [gains, *weights]
    in_specs += [_layer_spec((1, D_MODEL), layer), _const_spec((D_MODEL, D_FF)),
                 _const_spec((D_MODEL, D_FF)), _const_spec((D_FF, D_MODEL))]
    out_shape = [jax.ShapeDtypeStruct((n, D_MODEL), F32)]
    out_specs = [row_spec]
    for w in cast_from:
        rows, cols = w.shape[1] * per_slab // steps, w.shape[2]
        args.append(w)
        in_specs.append(pl.BlockSpec((1, rows, cols), lambda i: (cast_layer, i // per_slab, 0)))
        out_shape.append(jax.ShapeDtypeStruct(w.shape[1:], BF16))
        out_specs.append(pl.BlockSpec((rows, cols), lambda i: (i // per_slab, 0)))
    outs = pl.pallas_call(
        functools.partial(_ffn_body, n_lead=3 if lead else 0, n_cast=len(cast_from)),
        out_shape=out_shape,
        grid=(steps,),
        in_specs=in_specs,
        out_specs=out_specs,
        compiler_params=pltpu.CompilerParams(dimension_semantics=("arbitrary",),
                                             vmem_limit_bytes=VMEM_LIMIT),
        name="outproj_ffn" if lead else "ffn",
    )(*args)
    return outs[0], tuple(outs[1:])


def _proj_body(x_ref, gain_ref, wt_ref, wr_ref, qn_ref, kn_ref, cos_ref, sin_ref,
               qt_ref, k_ref, vt_ref, rest_ref):
    h = _rms_rows(x_ref[0], gain_ref[...]).astype(BF16)
    rest_ref[0] = jnp.dot(h, wr_ref[...], preferred_element_type=F32)
    qkvt = lax.dot_general(wt_ref[...], h, (((1,), (1,)), ((), ())), preferred_element_type=F32)
    cos = cos_ref[...]
    sin = sin_ref[...]

    def norm_rope(t, gain_col):
        ms = jnp.mean(t * t, axis=0, keepdims=True)
        tn = t * lax.rsqrt(ms + EPS) * gain_col
        x1, x2 = tn[:HALF], tn[HALF:]
        return jnp.concatenate([x1 * cos - x2 * sin, x2 * cos + x1 * sin], axis=0)

    n_blk = PROJ_ROWS // MOBA_BLOCK
    for p in range(HEAD_PAIRS):
        lo = p * PAIR_DIM
        q_pair = jnp.concatenate(
            [norm_rope(qkvt[lo + e * HEAD_DIM:lo + (e + 1) * HEAD_DIM], qn_ref[...]) for e in range(2)],
            axis=0)
        k_pair = jnp.concatenate(
            [norm_rope(qkvt[ATTN_WIDTH + lo + e * HEAD_DIM:ATTN_WIDTH + lo + (e + 1) * HEAD_DIM],
                       kn_ref[...]) for e in range(2)], axis=0)
        v_pair = qkvt[2 * ATTN_WIDTH + lo:2 * ATTN_WIDTH + lo + PAIR_DIM].astype(BF16)
        k_ref[0, :, lo:lo + PAIR_DIM] = k_pair.T.astype(BF16)
        for b in range(n_blk):
            qt_ref[0, p, b] = q_pair[:, b * MOBA_BLOCK:(b + 1) * MOBA_BLOCK]
            vt_ref[0, p, b] = v_pair[:, b * MOBA_BLOCK:(b + 1) * MOBA_BLOCK]


def _proj(x3d, gain_row, wt, wr, qn_col, kn_col, cos_t, sin_t):
    bsz, seq, _ = x3d.shape
    nb = seq // MOBA_BLOCK
    n_blk = PROJ_ROWS // MOBA_BLOCK
    blk_spec = pl.BlockSpec((1, HEAD_PAIRS, n_blk, PAIR_DIM, MOBA_BLOCK), lambda b, i: (b, 0, i, 0, 0))
    return pl.pallas_call(
        _proj_body,
        out_shape=(jax.ShapeDtypeStruct((bsz, HEAD_PAIRS, nb, PAIR_DIM, MOBA_BLOCK), F32),
                   jax.ShapeDtypeStruct((bsz, seq, ATTN_WIDTH), BF16),
                   jax.ShapeDtypeStruct((bsz, HEAD_PAIRS, nb, PAIR_DIM, MOBA_BLOCK), BF16),
                   jax.ShapeDtypeStruct((bsz, seq, REST_WIDTH), F32)),
        grid=(bsz, seq // PROJ_ROWS),
        in_specs=[pl.BlockSpec((1, PROJ_ROWS, D_MODEL), lambda b, i: (b, i, 0)),
                  _const_spec((1, D_MODEL)),
                  _const_spec((3 * ATTN_WIDTH, D_MODEL)),
                  _const_spec((D_MODEL, REST_WIDTH)),
                  _const_spec((HEAD_DIM, 1)), _const_spec((HEAD_DIM, 1)),
                  pl.BlockSpec((HALF, PROJ_ROWS), lambda b, i: (0, i)),
                  pl.BlockSpec((HALF, PROJ_ROWS), lambda b, i: (0, i))],
        out_specs=(blk_spec,
                   pl.BlockSpec((1, PROJ_ROWS, ATTN_WIDTH), lambda b, i: (b, i, 0)),
                   blk_spec,
                   pl.BlockSpec((1, PROJ_ROWS, REST_WIDTH), lambda b, i: (b, i, 0))),
        compiler_params=pltpu.CompilerParams(dimension_semantics=("arbitrary", "arbitrary"),
                                             vmem_limit_bytes=VMEM_LIMIT),
        name="proj",
    )(x3d, gain_row, wt, wr, qn_col, kn_col, cos_t, sin_t)


def _attn_body(qt_ref, k_ref, vt_ref, y_ref, kmean_scr, bias_scr, *s_scrs, nb):
    i = pl.program_id(2)
    width = ATTN_PAIRS * PAIR_DIM

    @pl.when(i == 0)
    def _():
        for j in range(nb):
            kb = k_ref[0, j * MOBA_BLOCK:(j + 1) * MOBA_BLOCK, :].astype(F32)
            kmean_scr[j:j + 1, :] = jnp.sum(kb, axis=0, keepdims=True) * (1.0 / MOBA_BLOCK)

    kmean = kmean_scr[...]
    lane_p = lax.broadcasted_iota(jnp.int32, (nb, PAIR_DIM), 1)
    qrow = lax.broadcasted_iota(jnp.int32, (PAIR_DIM, MOBA_BLOCK), 0)
    blk = lax.broadcasted_iota(jnp.int32, (nb, MOBA_BLOCK), 0)
    valid = blk < i

    heads = [(p, e) for p in range(ATTN_PAIRS) for e in range(2)]

    def split(v):
        hi = v.astype(BF16)
        return hi, (v - hi.astype(F32)).astype(BF16)

    gates = []
    for p in range(ATTN_PAIRS):
        q_hi, q_lo = split(qt_ref[0, p, 0])
        km_pair = kmean[:, p * PAIR_DIM:(p + 1) * PAIR_DIM]
        km_hi, km_lo = split(jnp.concatenate(
            [jnp.where((lane_p >= e * HEAD_DIM) & (lane_p < (e + 1) * HEAD_DIM), km_pair, 0.0)
             for e in range(2)], axis=0))
        gates.append(jnp.dot(km_hi, q_hi, preferred_element_type=F32)
                     + jnp.dot(km_lo, q_hi, preferred_element_type=F32)
                     + jnp.dot(km_hi, q_lo, preferred_element_type=F32))

    q_heads = []
    for h, (p, e) in enumerate(heads):
        qt = qt_ref[0, p, 0]
        gate = gates[p][e * nb:(e + 1) * nb]
        sel = jnp.zeros((nb, MOBA_BLOCK), F32)
        for _ in range(MOBA_TOPK):
            cand = jnp.where(valid & (sel == 0.0), gate, -jnp.inf)
            best = jnp.max(cand, axis=0, keepdims=True)
            is_best = (cand == best) & (best > -jnp.inf)
            first = jnp.min(jnp.where(is_best, blk, nb), axis=0, keepdims=True)
            sel = jnp.where(blk == first, 1.0, sel)
        bias_scr[h] = jnp.where(sel > 0.0, 0.0, MASK_BIAS)
        in_rows = (qrow >= e * HEAD_DIM) & (qrow < (e + 1) * HEAD_DIM)
        q_heads.append(jnp.where(in_rows, qt * (HEAD_DIM ** -0.5 * LOG2_E), 0.0).astype(BF16))

    ones_rows = jnp.ones((ONES_ROWS, MOBA_BLOCK), BF16)

    def values_ext(j, h):
        p, e = heads[h]
        return jnp.concatenate([vt_ref[0, p, j][e * HEAD_DIM:(e + 1) * HEAD_DIM, :], ones_rows], axis=0)

    def scores(j, h):
        p = heads[h][0]
        kj = k_ref[0, pl.ds(pl.multiple_of(j * MOBA_BLOCK, MOBA_BLOCK), MOBA_BLOCK),
                   p * PAIR_DIM:(p + 1) * PAIR_DIM]
        return jnp.dot(kj, q_heads[h], preferred_element_type=F32)

    n_heads = len(heads)
    init = []
    for h in range(n_heads):
        s = scores(0, h)
        s_scrs[h][0] = s
        init += [jnp.full((1, MOBA_BLOCK), NEG_INF, F32), jnp.max(s, axis=0, keepdims=True),
                 jnp.zeros((HEAD_DIM + ONES_ROWS, MOBA_BLOCK), F32)]

    def one_block(j, slot, carry):
        out = []
        for h in range(n_heads):
            m, cmax, acc = carry[3 * h:3 * h + 3]
            brow = bias_scr[h, pl.ds(j, 1), :]
            m_new = jnp.maximum(m, cmax + brow)
            p = jnp.exp2(s_scrs[h][slot] + (brow - m_new)).astype(BF16)
            acc = jnp.exp2(m - m_new) * acc + jnp.dot(values_ext(j, h), p, preferred_element_type=F32)
            s_next = scores(j + 1, h)
            s_scrs[h][1 - slot] = s_next
            out += [m_new, jnp.max(s_next, axis=0, keepdims=True), acc]
        return tuple(out)

    def body(t, carry):
        for u in range(ATTN_UNROLL):
            carry = one_block(ATTN_UNROLL * t + u, u & 1, carry)
        return carry

    n_unrolled = lax.shift_right_logical(i, ATTN_UNROLL.bit_length() - 1)
    mid = lax.fori_loop(0, n_unrolled, body, tuple(init))
    fin = lax.fori_loop(n_unrolled * ATTN_UNROLL, i, lambda j, c: one_block(j, j & 1, c), mid)

    key_pos = lax.broadcasted_iota(jnp.int32, (MOBA_BLOCK, MOBA_BLOCK), 0)
    q_pos = lax.broadcasted_iota(jnp.int32, (MOBA_BLOCK, MOBA_BLOCK), 1)
    causal_bias = jnp.where(key_pos <= q_pos, 0.0, MASK_BIAS)
    outs = []
    for h in range(n_heads):
        m, _, acc = fin[3 * h:3 * h + 3]
        s = s_scrs[h][i & 1] + causal_bias
        m_new = jnp.maximum(m, jnp.max(s, axis=0, keepdims=True))
        p = jnp.exp2(s - m_new).astype(BF16)
        acc = jnp.exp2(m - m_new) * acc + jnp.dot(values_ext(i, h), p, preferred_element_type=F32)
        outs.append(acc[:HEAD_DIM] / acc[HEAD_DIM:HEAD_DIM + 1])
    for p in range(ATTN_PAIRS):
        o_pair = jnp.concatenate(outs[2 * p:2 * p + 2], axis=0)
        y_ref[0, :, p * PAIR_DIM:(p + 1) * PAIR_DIM] = o_pair.T.astype(BF16)


def _attn(qt, k, vt):
    bsz, _, nb, _, _ = qt.shape
    assert ATTN_UNROLL % 2 == 0
    seq = nb * MOBA_BLOCK
    width = ATTN_PAIRS * PAIR_DIM
    n_heads = 2 * ATTN_PAIRS
    return pl.pallas_call(
        functools.partial(_attn_body, nb=nb),
        out_shape=jax.ShapeDtypeStruct((bsz, seq, ATTN_WIDTH), BF16),
        grid=(bsz, HEAD_PAIRS // ATTN_PAIRS, nb),
        in_specs=[pl.BlockSpec((1, ATTN_PAIRS, 1, PAIR_DIM, MOBA_BLOCK), lambda b, p, i: (b, p, i, 0, 0)),
                  pl.BlockSpec((1, seq, width), lambda b, p, i: (b, 0, p)),
                  pl.BlockSpec((1, ATTN_PAIRS, nb, PAIR_DIM, MOBA_BLOCK),
                               lambda b, p, i: (b, p, 0, 0, 0))],
        out_specs=pl.BlockSpec((1, MOBA_BLOCK, width), lambda b, p, i: (b, i, p)),
        scratch_shapes=[pltpu.VMEM((nb, width), F32),
                        pltpu.VMEM((n_heads, nb, MOBA_BLOCK), F32)]
        + [pltpu.VMEM((2, MOBA_BLOCK, MOBA_BLOCK), F32) for _ in range(n_heads)],
        compiler_params=pltpu.CompilerParams(
            dimension_semantics=("arbitrary", "arbitrary", "arbitrary"),
            vmem_limit_bytes=VMEM_LIMIT),
        name="moba_attn",
    )(qt, k, vt)


def _mix_chunk(cols, pos0, st, u_prev, lb, onorm, poolw, pscale, tri, head_ones, head_mask):
    rows, w = MIX_ROWS, HGRN_WIDTH
    u = cols(0, POOL_WIDTH)
    qh = cols(POOL_WIDTH, POOL_WIDTH + w)
    z = cols(POOL_WIDTH + w, POOL_WIDTH + 2 * w)
    value = cols(POOL_WIDTH + 2 * w, POOL_WIDTH + 3 * w)
    og = cols(POOL_WIDTH + 3 * w, POOL_WIDTH + 4 * w)

    lane = lax.broadcasted_iota(jnp.int32, (rows, w), 1)
    row = lax.broadcasted_iota(jnp.int32, (rows, w), 0)

    uext = jnp.concatenate([u_prev, u], axis=0)
    sums = []
    acc = uext
    for shift in (1, 2, 4, 8):
        acc = acc + pltpu.roll(acc, shift, 0)
        sums.append(acc[POOL_HALO:])
    in_g0, in_g01, in_g012 = (lane < g * POOL_GROUP_DIM for g in (1, 2, 3))
    win_sum = jnp.where(in_g0, sums[0], jnp.where(in_g01, sums[1], jnp.where(in_g012, sums[2], sums[3])))
    window = jnp.where(in_g0, POOL_WINDOWS[0],
                       jnp.where(in_g01, POOL_WINDOWS[1],
                                 jnp.where(in_g012, POOL_WINDOWS[2], POOL_WINDOWS[3])))
    count = jnp.minimum(pos0 + row + 1, window).astype(F32)
    diff = win_sum / count - u
    y_pool = jnp.dot(diff.astype(BF16), poolw, preferred_element_type=F32) * pscale

    log_sig = jnp.minimum(z, 0.0) - jnp.log1p(jnp.exp(-jnp.abs(z)))
    a_term = jnp.log(jnp.maximum(lb, LB_FLOOR))
    b_term = jnp.log1p(-lb) + log_sig
    log_f = jnp.maximum(a_term, b_term) + jnp.log1p(jnp.exp(-jnp.abs(a_term - b_term)))
    key = (1.0 - lb) * jax.nn.sigmoid(-z)
    query = _silu(qh) * (HGRN_HEAD_DIM ** -0.5)

    f_hi = log_f.astype(BF16)
    f_rest = log_f - f_hi.astype(F32)
    f_mid = f_rest.astype(BF16)
    f_lo = (f_rest - f_mid.astype(F32)).astype(BF16)
    cum = (jnp.dot(tri, f_hi, preferred_element_type=F32) + jnp.dot(tri, f_mid, preferred_element_type=F32)
           + jnp.dot(tri, f_lo, preferred_element_type=F32)) * LOG2_E
    n_sub = rows // SUB
    last = [cum[(a + 1) * SUB - 1:(a + 1) * SUB] for a in range(n_sub)]
    tot = jnp.concatenate([jnp.broadcast_to(r, (SUB, w)) for r in last], axis=0)
    k_dec = (key * jnp.exp2(tot - cum)).astype(BF16)
    q_dec = (query * jnp.exp2(cum)).astype(BF16)
    value_b = value.astype(BF16)

    sub_row = lax.broadcasted_iota(jnp.int32, (SUB, w), 0)
    half_row = lax.broadcasted_iota(jnp.int32, (SUB // 2, w), 0) + SUB // 2

    outs = []
    for a in range(n_sub):
        rs = slice(a * SUB, (a + 1) * SUB)
        o_inter = lax.dot_general(q_dec[rs], st.astype(BF16), (((1,), (1,)), ((), ())),
                                  preferred_element_type=F32)
        cum_a, q_a, k_a, v_a = cum[rs], query[rs], key[rs], value[rs]
        pieces = []
        for s in range(SUB):
            if s < SUB // 2:
                decay = jnp.exp2(jnp.minimum(cum_a - cum_a[s:s + 1], 0.0))
                pieces.append(jnp.where(sub_row >= s, decay * (q_a * k_a[s:s + 1]), 0.0))
            else:
                decay = jnp.exp2(jnp.minimum(cum_a[SUB // 2:] - cum_a[s:s + 1], 0.0))
                pieces.append(jnp.where(half_row >= s, decay * (q_a[SUB // 2:] * k_a[s:s + 1]), 0.0))
        e_all = jnp.concatenate(pieces, axis=0).astype(BF16)
        scores = jnp.dot(e_all, head_ones, preferred_element_type=F32)
        o_top = jnp.zeros((SUB // 2, w), F32)
        o_bot = jnp.zeros((SUB // 2, w), F32)
        for s in range(SUB):
            if s < SUB // 2:
                o_top = o_top + scores[s * SUB:s * SUB + SUB // 2] * v_a[s:s + 1]
                o_bot = o_bot + scores[s * SUB + SUB // 2:(s + 1) * SUB] * v_a[s:s + 1]
            else:
                base = (SUB // 2) * SUB + (s - SUB // 2) * (SUB // 2)
                o_bot = o_bot + scores[base:base + SUB // 2] * v_a[s:s + 1]
        outs.append(o_inter + jnp.concatenate([o_top, o_bot], axis=0))
        upd_t = lax.dot_general(value_b[rs], k_dec[rs], (((0,), (0,)), ((), ())),
                                preferred_element_type=F32)
        st = jnp.exp2(last[a]) * st + upd_t * head_mask

    o = jnp.concatenate(outs, axis=0)
    sq = o * o
    sq_hi = sq.astype(BF16)
    sq_lo = (sq - sq_hi.astype(F32)).astype(BF16)
    ms = (jnp.dot(sq_hi, head_ones, preferred_element_type=F32)
          + jnp.dot(sq_lo, head_ones, preferred_element_type=F32)) * (1.0 / HGRN_HEAD_DIM)
    y_hgrn = o * lax.rsqrt(ms + EPS) * onorm * _silu(og)
    return y_pool.astype(BF16), y_hgrn.astype(BF16), st, u[rows - POOL_HALO:]


def _mix_constants():
    r = np.arange(MIX_ROWS)
    tri = ((r[:, None] // SUB == r[None, :] // SUB) & (r[None, :] <= r[:, None])).astype(np.float32)
    c = np.arange(HGRN_WIDTH) // HGRN_HEAD_DIM
    same_head = (c[:, None] == c[None, :]).astype(np.float32)
    return jnp.asarray(tri, BF16), jnp.asarray(same_head, BF16), jnp.asarray(same_head, F32)


N_MIX_FIXED = 16


def _mix_ffn_body(*refs, n_cast, tiles_per_seq, n_tiles):
    (x_ref, ya_ref, wo_ref, gain_ref, wg_ref, wu_ref, wd_ref, rest0_ref, rest_ref,
     lb_ref, onorm_ref, poolw_ref, pscale_ref, tri_ref, hones_ref, hmask_ref) = refs[:N_MIX_FIXED]
    src = refs[N_MIX_FIXED:N_MIX_FIXED + n_cast]
    o_ref = refs[N_MIX_FIXED + n_cast]
    dst = refs[N_MIX_FIXED + n_cast + 1:N_MIX_FIXED + 2 * n_cast + 1]
    y_scr, st_scr, uprev_scr = refs[N_MIX_FIXED + 2 * n_cast + 1:]
    j = pl.program_id(0)
    tile = x_ref.shape[0]

    n_chunks = tile // MIX_ROWS

    def mix_chunks(r_ref, t, slot):
        consts = (lb_ref[...], onorm_ref[...], poolw_ref[...], pscale_ref[...], tri_ref[...],
                  hones_ref[...], hmask_ref[...])
        t_in_seq = t & (tiles_per_seq - 1)
        starts_seq = t_in_seq == 0
        state = [jnp.where(starts_seq, 0.0, st_scr[...]), jnp.where(starts_seq, 0.0, uprev_scr[...])]

        def chunk(c):
            rows = slice(c * MIX_ROWS, (c + 1) * MIX_ROWS)
            y_pool, y_hgrn, state[0], state[1] = _mix_chunk(
                lambda lo, hi: r_ref[0, rows, lo:hi], t_in_seq * tile + c * MIX_ROWS, *state, *consts)
            y_scr[slot, rows, 0:POOL_WIDTH] = y_pool
            y_scr[slot, rows, POOL_WIDTH:] = y_hgrn
            if c == n_chunks - 1:
                st_scr[...] = state[0]
                uprev_scr[...] = state[1]

        return chunk

    @pl.when(j == 0)
    def _():
        first = mix_chunks(rest0_ref, 0, 0)
        for c in range(n_chunks):
            first(c)

    slot = j & 1
    x = x_ref[...]
    x = x + jnp.dot(ya_ref[...], wo_ref[0, :ATTN_WIDTH, :], preferred_element_type=F32)
    x = x + jnp.dot(y_scr[slot], wo_ref[0, ATTN_WIDTH:, :], preferred_element_type=F32)
    nxt = mix_chunks(rest_ref, jnp.minimum(j + 1, n_tiles - 1), 1 - slot)
    at = {(k * N_FF_CHUNKS) // n_chunks: k for k in range(n_chunks)}
    o_ref[...] = _swiglu_half_step(x, gain_ref[0], wg_ref, wu_ref, wd_ref,
                                   between=lambda c: nxt(at[c]) if c in at else None)
    for s_ref, d_ref in zip(src, dst):
        d_ref[...] = s_ref[0].astype(BF16)


def _mix_ffn(x2d, y_attn, rest, gains, layer, weights, w_out, mix_rows, cast_from=(), cast_layer=0):
    n = x2d.shape[0]
    bsz, seq, _ = rest.shape
    tile = MIX_FFN_ROWS
    n_tiles = n // tile
    tiles_per_seq = seq // tile
    assert tiles_per_seq & (tiles_per_seq - 1) == 0 and tile % MIX_ROWS == 0
    per_slab = max(n_tiles // CAST_SLABS, 1)
    rest_tiles = rest.reshape(n_tiles, tile, REST_WIDTH)
    tri, head_ones, head_mask = _mix_constants()
    row_spec = pl.BlockSpec((tile, D_MODEL), lambda i: (i, 0))
    args = [x2d, y_attn, w_out, gains, *weights, rest_tiles, rest_tiles, *mix_rows, tri, head_ones,
            head_mask]
    in_specs = [row_spec, pl.BlockSpec((tile, ATTN_WIDTH), lambda i: (i, 0)),
                _layer_spec((D_MODEL, D_MODEL), layer), _layer_spec((1, D_MODEL), layer),
                _const_spec((D_MODEL, D_FF)), _const_spec((D_MODEL, D_FF)), _const_spec((D_FF, D_MODEL)),
                _const_spec((1, tile, REST_WIDTH)),
                pl.BlockSpec((1, tile, REST_WIDTH), lambda i: (jnp.minimum(i + 1, n_tiles - 1), 0, 0)),
                _const_spec((1, HGRN_WIDTH)), _const_spec((1, HGRN_WIDTH)),
                _const_spec((POOL_WIDTH, POOL_WIDTH)), _const_spec((1, POOL_WIDTH)),
                _const_spec((MIX_ROWS, MIX_ROWS)), _const_spec((HGRN_WIDTH, HGRN_WIDTH)),
                _const_spec((HGRN_WIDTH, HGRN_WIDTH))]
    assert len(in_specs) == N_MIX_FIXED
    out_shape = [jax.ShapeDtypeStruct((n, D_MODEL), F32)]
    out_specs = [row_spec]
    for w in cast_from:
        rows, cols = w.shape[1] * per_slab // n_tiles, w.shape[2]
        args.append(w)
        in_specs.append(pl.BlockSpec((1, rows, cols), lambda i: (cast_layer, i // per_slab, 0)))
        out_shape.append(jax.ShapeDtypeStruct(w.shape[1:], BF16))
        out_specs.append(pl.BlockSpec((rows, cols), lambda i: (i // per_slab, 0)))
    outs = pl.pallas_call(
        functools.partial(_mix_ffn_body, n_cast=len(cast_from), tiles_per_seq=tiles_per_seq,
                          n_tiles=n_tiles),
        out_shape=out_shape,
        grid=(n_tiles,),
        in_specs=in_specs,
        out_specs=out_specs,
        scratch_shapes=[pltpu.VMEM((2, tile, POOL_WIDTH + HGRN_WIDTH), BF16),
                        pltpu.VMEM((HGRN_WIDTH, HGRN_WIDTH), F32),
                        pltpu.VMEM((POOL_HALO, POOL_WIDTH), F32)],
        compiler_params=pltpu.CompilerParams(dimension_semantics=("arbitrary",),
                                             vmem_limit_bytes=VMEM_LIMIT),
        name="mix_outproj_ffn",
    )(*args)
    return outs[0], tuple(outs[1:])


def _block_diag(blocks):
    g, c, e = blocks.shape
    out = jnp.zeros((g * c, g * e), blocks.dtype)
    for i in range(g):
        out = out.at[i * c:(i + 1) * c, i * e:(i + 1) * e].set(blocks[i])
    return out


def kernel(x, ffn1_norm, ffn1_w_gate, ffn1_w_up, ffn1_w_down, mix_norm, w_in, q_norm, k_norm,
           pool_w, pool_scale, hgrn_lb, hgrn_out_norm, w_out, ffn2_norm, ffn2_w_gate, ffn2_w_up,
           ffn2_w_down):
    bsz, seq, d = x.shape
    n = bsz * seq

    inv_freq = ROPE_THETA ** (-jnp.arange(HALF, dtype=F32) / HALF)
    ang = jnp.arange(seq).astype(F32)[:, None] * inv_freq[None, :]
    cos_t, sin_t = jnp.cos(ang).T, jnp.sin(ang).T

    lb_soft = jax.nn.softmax(hgrn_lb.astype(F32), axis=0)
    lower_bounds = jnp.concatenate([jnp.zeros_like(lb_soft[:1]), jnp.cumsum(lb_soft[:-1], axis=0)],
                                   axis=0)

    ffn1_f32 = (ffn1_w_gate, ffn1_w_up, ffn1_w_down)
    ffn2_f32 = (ffn2_w_gate, ffn2_w_up, ffn2_w_down)
    gains1, gains2 = ffn1_norm[:, None, :], ffn2_norm[:, None, :]
    w1 = tuple(w[0].astype(BF16) for w in ffn1_f32)
    w_in_b = w_in.astype(BF16)
    w_qkv_t = jnp.swapaxes(w_in_b[:, :, :3 * ATTN_WIDTH], 1, 2)
    w_rest = w_in_b[:, :, 3 * ATTN_WIDTH:]
    w_out_b = w_out.astype(BF16)

    x2d = x.reshape(n, d)
    for l in range(DEPTH):
        x2d, w2 = _ffn(x2d, gains1, l, w1, cast_from=ffn2_f32, cast_layer=l)
        qt, k, vt, rest = _proj(x2d.reshape(bsz, seq, d), mix_norm[l][None, :], w_qkv_t[l], w_rest[l],
                                q_norm[l][:, None], k_norm[l][:, None], cos_t, sin_t)
        y_attn = _attn(qt, k, vt)
        mix_rows = (lower_bounds[l][None, :], jnp.tile(hgrn_out_norm[l], 4)[None, :],
                    _block_diag(pool_w[l]).astype(BF16), pool_scale[l][None, :])
        last = l == DEPTH - 1
        x2d, w1 = _mix_ffn(x2d, y_attn.reshape(n, ATTN_WIDTH), rest, gains2, l, w2, w_out_b, mix_rows,
                           cast_from=() if last else ffn1_f32, cast_layer=0 if last else l + 1)
    return x2d.reshape(bsz, seq, d)
```

```python
import functools

import jax
import jax.numpy as jnp
import numpy as np
from jax import lax
from jax.experimental import pallas as pl
from jax.experimental.pallas import tpu as pltpu

F32 = jnp.float32
BF16 = jnp.bfloat16

D_MODEL = 1024
DEPTH = 4
ATTN_HEADS = 8
HEAD_DIM = 64
HALF = HEAD_DIM // 2
ATTN_WIDTH = ATTN_HEADS * HEAD_DIM
HEAD_PAIRS = ATTN_HEADS // 2
PAIR_DIM = 2 * HEAD_DIM
POOL_WINDOWS = (2, 4, 8, 16)
POOL_GROUP_DIM = 64
POOL_WIDTH = 256
HGRN_HEAD_DIM = 64
HGRN_WIDTH = 256
REST_WIDTH = POOL_WIDTH + 4 * HGRN_WIDTH
D_FF = 2816
FF_CHUNK = 256
N_FF_CHUNKS = D_FF // FF_CHUNK
MOBA_BLOCK = 256
MOBA_TOPK = 3
ROPE_THETA = 10000.0
EPS = 1e-6
NEG_INF = -1e30
LB_FLOOR = 1e-20

FFN_ROWS = 1024
CAST_SLABS = 16
PROJ_ROWS = 512
MIX_ROWS = 128
SUB = 16
POOL_HALO = 16
ATTN_PAIRS = 4
ATTN_UNROLL = 4
ONES_ROWS = 16
MASK_BIAS = 2 * NEG_INF
LOG2_E = 1.4426950408889634
VMEM_LIMIT = 56 * 1024 * 1024


def _rms_rows(x, gain_row):
    ms = jnp.mean(x * x, axis=-1, keepdims=True)
    return x * lax.rsqrt(ms + EPS) * gain_row


def _silu(x):
    return x * jax.nn.sigmoid(x)


def _swiglu_half_step(x, gain_row, wg_ref, wu_ref, wd_ref):
    h = _rms_rows(x, gain_row).astype(BF16)
    acc = jnp.zeros_like(x)
    for c in range(N_FF_CHUNKS):
        cols = slice(c * FF_CHUNK, (c + 1) * FF_CHUNK)
        g = jnp.dot(h, wg_ref[:, cols], preferred_element_type=F32)
        u = jnp.dot(h, wu_ref[:, cols], preferred_element_type=F32)
        acc = acc + jnp.dot((_silu(g) * u).astype(BF16), wd_ref[cols, :],
                            preferred_element_type=F32)
    return x + 0.5 * acc


def _ffn_body(*refs, n_lead, n_cast):
    x = refs[0][...]
    if n_lead:
        ya_ref, yr_ref, wo_ref = refs[1:4]
        x = x + jnp.dot(ya_ref[...], wo_ref[0, :ATTN_WIDTH, :], preferred_element_type=F32)
        x = x + jnp.dot(yr_ref[...], wo_ref[0, ATTN_WIDTH:, :], preferred_element_type=F32)
    gain_ref, wg_ref, wu_ref, wd_ref = refs[1 + n_lead:5 + n_lead]
    src = refs[5 + n_lead:5 + n_lead + n_cast]
    o_ref = refs[5 + n_lead + n_cast]
    dst = refs[6 + n_lead + n_cast:]
    o_ref[...] = _swiglu_half_step(x, gain_ref[0], wg_ref, wu_ref, wd_ref)
    for s_ref, d_ref in zip(src, dst):
        d_ref[...] = s_ref[0].astype(BF16)


def _const_spec(shape):
    return pl.BlockSpec(shape, lambda *_: (0,) * len(shape), pipeline_mode=pl.Buffered(1))


def _layer_spec(shape, layer):
    return pl.BlockSpec((1,) + shape, lambda *_: (layer,) + (0,) * len(shape),
                        pipeline_mode=pl.Buffered(1))


def _ffn(x2d, gains, layer, weights, lead=(), w_out=None, cast_from=(), cast_layer=0):
    n = x2d.shape[0]
    tile = FFN_ROWS
    steps = n // tile
    per_slab = max(steps // CAST_SLABS, 1)
    row_spec = pl.BlockSpec((tile, D_MODEL), lambda i: (i, 0))
    half_spec = pl.BlockSpec((tile, ATTN_WIDTH), lambda i: (i, 0))
    args, in_specs = [x2d], [row_spec]
    if lead:
        args += [*lead, w_out]
        in_specs += [half_spec, half_spec, _layer_spec((D_MODEL, D_MODEL), layer)]
    args += [gains, *weights]
    in_specs += [_layer_spec((1, D_MODEL), layer), _const_spec((D_MODEL, D_FF)),
                 _const_spec((D_MODEL, D_FF)), _const_spec((D_FF, D_MODEL))]
    out_shape = [jax.ShapeDtypeStruct((n, D_MODEL), F32)]
    out_specs = [row_spec]
    for w in cast_from:
        rows, cols = w.shape[1] * per_slab // steps, w.shape[2]
        args.append(w)
        in_specs.append(pl.BlockSpec((1, rows, cols), lambda i: (cast_layer, i // per_slab, 0)))
        out_shape.append(jax.ShapeDtypeStruct(w.shape[1:], BF16))
        out_specs.append(pl.BlockSpec((rows, cols), lambda i: (i // per_slab, 0)))
    outs = pl.pallas_call(
        functools.partial(_ffn_body, n_lead=3 if lead else 0, n_cast=len(cast_from)),
        out_shape=out_shape,
        grid=(steps,),
        in_specs=in_specs,
        out_specs=out_specs,
        compiler_params=pltpu.CompilerParams(dimension_semantics=("arbitrary",),
                                             vmem_limit_bytes=VMEM_LIMIT),
        name="outproj_ffn" if lead else "ffn",
    )(*args)
    return outs[0], tuple(outs[1:])


def _proj_body(x_ref, gain_ref, wt_ref, wr_ref, qn_ref, kn_ref, cos_ref, sin_ref,
               qt_ref, k_ref, vt_ref, rest_ref):
    h = _rms_rows(x_ref[0], gain_ref[...]).astype(BF16)
    rest_ref[0] = jnp.dot(h, wr_ref[...], preferred_element_type=F32)
    qkvt = lax.dot_general(wt_ref[...], h, (((1,), (1,)), ((), ())), preferred_element_type=F32)
    cos = cos_ref[...]
    sin = sin_ref[...]

    def norm_rope(t, gain_col):
        ms = jnp.mean(t * t, axis=0, keepdims=True)
        tn = t * lax.rsqrt(ms + EPS) * gain_col
        x1, x2 = tn[:HALF], tn[HALF:]
        return jnp.concatenate([x1 * cos - x2 * sin, x2 * cos + x1 * sin], axis=0)

    n_blk = PROJ_ROWS // MOBA_BLOCK
    for p in range(HEAD_PAIRS):
        lo = p * PAIR_DIM
        q_pair = jnp.concatenate(
            [norm_rope(qkvt[lo + e * HEAD_DIM:lo + (e + 1) * HEAD_DIM], qn_ref[...]) for e in range(2)],
            axis=0)
        k_pair = jnp.concatenate(
            [norm_rope(qkvt[ATTN_WIDTH + lo + e * HEAD_DIM:ATTN_WIDTH + lo + (e + 1) * HEAD_DIM],
                       kn_ref[...]) for e in range(2)], axis=0)
        v_pair = qkvt[2 * ATTN_WIDTH + lo:2 * ATTN_WIDTH + lo + PAIR_DIM].astype(BF16)
        k_ref[0, :, lo:lo + PAIR_DIM] = k_pair.T.astype(BF16)
        for b in range(n_blk):
            qt_ref[0, p, b] = q_pair[:, b * MOBA_BLOCK:(b + 1) * MOBA_BLOCK]
            vt_ref[0, p, b] = v_pair[:, b * MOBA_BLOCK:(b + 1) * MOBA_BLOCK]


def _proj(x3d, gain_row, wt, wr, qn_col, kn_col, cos_t, sin_t):
    bsz, seq, _ = x3d.shape
    nb = seq // MOBA_BLOCK
    n_blk = PROJ_ROWS // MOBA_BLOCK
    blk_spec = pl.BlockSpec((1, HEAD_PAIRS, n_blk, PAIR_DIM, MOBA_BLOCK), lambda b, i: (b, 0, i, 0, 0))
    return pl.pallas_call(
        _proj_body,
        out_shape=(jax.ShapeDtypeStruct((bsz, HEAD_PAIRS, nb, PAIR_DIM, MOBA_BLOCK), F32),
                   jax.ShapeDtypeStruct((bsz, seq, ATTN_WIDTH), BF16),
                   jax.ShapeDtypeStruct((bsz, HEAD_PAIRS, nb, PAIR_DIM, MOBA_BLOCK), BF16),
                   jax.ShapeDtypeStruct((bsz, seq, REST_WIDTH), F32)),
        grid=(bsz, seq // PROJ_ROWS),
        in_specs=[pl.BlockSpec((1, PROJ_ROWS, D_MODEL), lambda b, i: (b, i, 0)),
                  _const_spec((1, D_MODEL)),
                  _const_spec((3 * ATTN_WIDTH, D_MODEL)),
                  _const_spec((D_MODEL, REST_WIDTH)),
                  _const_spec((HEAD_DIM, 1)), _const_spec((HEAD_DIM, 1)),
                  pl.BlockSpec((HALF, PROJ_ROWS), lambda b, i: (0, i)),
                  pl.BlockSpec((HALF, PROJ_ROWS), lambda b, i: (0, i))],
        out_specs=(blk_spec,
                   pl.BlockSpec((1, PROJ_ROWS, ATTN_WIDTH), lambda b, i: (b, i, 0)),
                   blk_spec,
                   pl.BlockSpec((1, PROJ_ROWS, REST_WIDTH), lambda b, i: (b, i, 0))),
        compiler_params=pltpu.CompilerParams(dimension_semantics=("arbitrary", "arbitrary"),
                                             vmem_limit_bytes=VMEM_LIMIT),
        name="proj",
    )(x3d, gain_row, wt, wr, qn_col, kn_col, cos_t, sin_t)


def _attn_body(qt_ref, k_ref, vt_ref, y_ref, kmean_scr, bias_scr, *s_scrs, nb):
    i = pl.program_id(2)
    width = ATTN_PAIRS * PAIR_DIM

    @pl.when(i == 0)
    def _():
        for j in range(nb):
            kb = k_ref[0, j * MOBA_BLOCK:(j + 1) * MOBA_BLOCK, :].astype(F32)
            kmean_scr[j:j + 1, :] = jnp.sum(kb, axis=0, keepdims=True) * (1.0 / MOBA_BLOCK)

    kmean = kmean_scr[...]
    lane_p = lax.broadcasted_iota(jnp.int32, (nb, PAIR_DIM), 1)
    qrow = lax.broadcasted_iota(jnp.int32, (PAIR_DIM, MOBA_BLOCK), 0)
    blk = lax.broadcasted_iota(jnp.int32, (nb, MOBA_BLOCK), 0)
    valid = blk < i

    heads = [(p, e) for p in range(ATTN_PAIRS) for e in range(2)]

    def split(v):
        hi = v.astype(BF16)
        return hi, (v - hi.astype(F32)).astype(BF16)

    gates = []
    for p in range(ATTN_PAIRS):
        q_hi, q_lo = split(qt_ref[0, p, 0])
        km_pair = kmean[:, p * PAIR_DIM:(p + 1) * PAIR_DIM]
        km_hi, km_lo = split(jnp.concatenate(
            [jnp.where((lane_p >= e * HEAD_DIM) & (lane_p < (e + 1) * HEAD_DIM), km_pair, 0.0)
             for e in range(2)], axis=0))
        gates.append(jnp.dot(km_hi, q_hi, preferred_element_type=F32)
                     + jnp.dot(km_lo, q_hi, preferred_element_type=F32)
                     + jnp.dot(km_hi, q_lo, preferred_element_type=F32))

    q_heads = []
    for h, (p, e) in enumerate(heads):
        qt = qt_ref[0, p, 0]
        gate = gates[p][e * nb:(e + 1) * nb]
        sel = jnp.zeros((nb, MOBA_BLOCK), F32)
        for _ in range(MOBA_TOPK):
            cand = jnp.where(valid & (sel == 0.0), gate, -jnp.inf)
            best = jnp.max(cand, axis=0, keepdims=True)
            is_best = (cand == best) & (best > -jnp.inf)
            first = jnp.min(jnp.where(is_best, blk, nb), axis=0, keepdims=True)
            sel = jnp.where(blk == first, 1.0, sel)
        bias_scr[h] = jnp.where(sel > 0.0, 0.0, MASK_BIAS)
        in_rows = (qrow >= e * HEAD_DIM) & (qrow < (e + 1) * HEAD_DIM)
        q_heads.append(jnp.where(in_rows, qt * (HEAD_DIM ** -0.5 * LOG2_E), 0.0).astype(BF16))

    ones_rows = jnp.ones((ONES_ROWS, MOBA_BLOCK), BF16)

    def values_ext(j, h):
        p, e = heads[h]
        return jnp.concatenate([vt_ref[0, p, j][e * HEAD_DIM:(e + 1) * HEAD_DIM, :], ones_rows], axis=0)

    def scores(j, h):
        p = heads[h][0]
        kj = k_ref[0, pl.ds(pl.multiple_of(j * MOBA_BLOCK, MOBA_BLOCK), MOBA_BLOCK),
                   p * PAIR_DIM:(p + 1) * PAIR_DIM]
        return jnp.dot(kj, q_heads[h], preferred_element_type=F32)

    n_heads = len(heads)
    acc_scrs = s_scrs[n_heads:]
    init = []
    for h in range(n_heads):
        s = scores(0, h)
        s_scrs[h][0] = s
        acc_scrs[h][...] = jnp.zeros((HEAD_DIM + ONES_ROWS, MOBA_BLOCK), F32)
        init += [jnp.full((1, MOBA_BLOCK), NEG_INF, F32), jnp.max(s, axis=0, keepdims=True)]

    def one_block(j, slot, carry):
        out = []
        for h in range(n_heads):
            m, cmax = carry[2 * h:2 * h + 2]
            brow = bias_scr[h, pl.ds(j, 1), :]
            m_new = jnp.maximum(m, cmax + brow)
            p = jnp.exp2(s_scrs[h][slot] + (brow - m_new)).astype(BF16)
            acc_scrs[h][...] = (jnp.exp2(m - m_new) * acc_scrs[h][...]
                                + jnp.dot(values_ext(j, h), p, preferred_element_type=F32))
            s_next = scores(j + 1, h)
            s_scrs[h][1 - slot] = s_next
            out += [m_new, jnp.max(s_next, axis=0, keepdims=True)]
        return tuple(out)

    def body(t, carry):
        for u in range(ATTN_UNROLL):
            carry = one_block(ATTN_UNROLL * t + u, u & 1, carry)
        return carry

    n_unrolled = lax.shift_right_logical(i, ATTN_UNROLL.bit_length() - 1)
    mid = lax.fori_loop(0, n_unrolled, body, tuple(init))
    fin = lax.fori_loop(n_unrolled * ATTN_UNROLL, i, lambda j, c: one_block(j, j & 1, c), mid)

    key_pos = lax.broadcasted_iota(jnp.int32, (MOBA_BLOCK, MOBA_BLOCK), 0)
    q_pos = lax.broadcasted_iota(jnp.int32, (MOBA_BLOCK, MOBA_BLOCK), 1)
    causal_bias = jnp.where(key_pos <= q_pos, 0.0, MASK_BIAS)
    outs = []
    for h in range(n_heads):
        m = fin[2 * h]
        s = s_scrs[h][i & 1] + causal_bias
        m_new = jnp.maximum(m, jnp.max(s, axis=0, keepdims=True))
        p = jnp.exp2(s - m_new).astype(BF16)
        acc = (jnp.exp2(m - m_new) * acc_scrs[h][...]
               + jnp.dot(values_ext(i, h), p, preferred_element_type=F32))
        outs.append(acc[:HEAD_DIM] / acc[HEAD_DIM:HEAD_DIM + 1])
    for p in range(ATTN_PAIRS):
        o_pair = jnp.concatenate(outs[2 * p:2 * p + 2], axis=0)
        y_ref[0, :, p * PAIR_DIM:(p + 1) * PAIR_DIM] = o_pair.T.astype(BF16)


def _attn(qt, k, vt):
    bsz, _, nb, _, _ = qt.shape
    assert ATTN_UNROLL % 2 == 0
    seq = nb * MOBA_BLOCK
    width = ATTN_PAIRS * PAIR_DIM
    n_heads = 2 * ATTN_PAIRS
    return pl.pallas_call(
        functools.partial(_attn_body, nb=nb),
        out_shape=jax.ShapeDtypeStruct((bsz, seq, ATTN_WIDTH), BF16),
        grid=(bsz, HEAD_PAIRS // ATTN_PAIRS, nb),
        in_specs=[pl.BlockSpec((1, ATTN_PAIRS, 1, PAIR_DIM, MOBA_BLOCK), lambda b, p, i: (b, p, i, 0, 0)),
                  pl.BlockSpec((1, seq, width), lambda b, p, i: (b, 0, p)),
                  pl.BlockSpec((1, ATTN_PAIRS, nb, PAIR_DIM, MOBA_BLOCK),
                               lambda b, p, i: (b, p, 0, 0, 0))],
        out_specs=pl.BlockSpec((1, MOBA_BLOCK, width), lambda b, p, i: (b, i, p)),
        scratch_shapes=[pltpu.VMEM((nb, width), F32),
                        pltpu.VMEM((n_heads, nb, MOBA_BLOCK), F32)]
        + [pltpu.VMEM((2, MOBA_BLOCK, MOBA_BLOCK), F32) for _ in range(n_heads)]
        + [pltpu.VMEM((HEAD_DIM + ONES_ROWS, MOBA_BLOCK), F32) for _ in range(n_heads)],
        compiler_params=pltpu.CompilerParams(
            dimension_semantics=("arbitrary", "arbitrary", "arbitrary"),
            vmem_limit_bytes=VMEM_LIMIT),
        name="moba_attn",
    )(qt, k, vt)


def _mix_body(r_ref, *rest):
    st_scr, uprev_scr = rest[-2:]
    t = pl.program_id(0)

    @pl.when(t == 0)
    def _():
        st_scr[...] = jnp.zeros_like(st_scr)
        uprev_scr[...] = jnp.zeros_like(uprev_scr)

    for b in range(r_ref.shape[0]):
        _mix_one(b, t, r_ref, *rest)


def _mix_one(b, t, r_ref, lb_ref, onorm_ref, poolw_ref, pscale_ref, tri_ref, hones_ref, hmask_ref,
             y_ref, st_scr, uprev_scr):
    rows, w = MIX_ROWS, HGRN_WIDTH
    u = r_ref[b, :, 0:POOL_WIDTH]
    qh = r_ref[b, :, POOL_WIDTH:POOL_WIDTH + w]
    z = r_ref[b, :, POOL_WIDTH + w:POOL_WIDTH + 2 * w]
    value = r_ref[b, :, POOL_WIDTH + 2 * w:POOL_WIDTH + 3 * w]
    og = r_ref[b, :, POOL_WIDTH + 3 * w:POOL_WIDTH + 4 * w]

    lane = lax.broadcasted_iota(jnp.int32, (rows, w), 1)
    row = lax.broadcasted_iota(jnp.int32, (rows, w), 0)

    uext = jnp.concatenate([uprev_scr[b], u], axis=0)
    sums = []
    acc = uext
    for shift in (1, 2, 4, 8):
        acc = acc + pltpu.roll(acc, shift, 0)
        sums.append(acc[POOL_HALO:])
    in_g0, in_g01, in_g012 = (lane < g * POOL_GROUP_DIM for g in (1, 2, 3))
    win_sum = jnp.where(in_g0, sums[0], jnp.where(in_g01, sums[1], jnp.where(in_g012, sums[2], sums[3])))
    window = jnp.where(in_g0, POOL_WINDOWS[0],
                       jnp.where(in_g01, POOL_WINDOWS[1],
                                 jnp.where(in_g012, POOL_WINDOWS[2], POOL_WINDOWS[3])))
    count = jnp.minimum(t * rows + row + 1, window).astype(F32)
    diff = win_sum / count - u
    y_pool = jnp.dot(diff.astype(BF16), poolw_ref[...], preferred_element_type=F32) * pscale_ref[...]
    uprev_scr[b] = u[rows - POOL_HALO:]

    lb = lb_ref[...]
    log_sig = jnp.minimum(z, 0.0) - jnp.log1p(jnp.exp(-jnp.abs(z)))
    a_term = jnp.log(jnp.maximum(lb, LB_FLOOR))
    b_term = jnp.log1p(-lb) + log_sig
    log_f = jnp.maximum(a_term, b_term) + jnp.log1p(jnp.exp(-jnp.abs(a_term - b_term)))
    key = (1.0 - lb) * jax.nn.sigmoid(-z)
    query = _silu(qh) * (HGRN_HEAD_DIM ** -0.5)

    tri = tri_ref[...]
    f_hi = log_f.astype(BF16)
    f_rest = log_f - f_hi.astype(F32)
    f_mid = f_rest.astype(BF16)
    f_lo = (f_rest - f_mid.astype(F32)).astype(BF16)
    cum = (jnp.dot(tri, f_hi, preferred_element_type=F32) + jnp.dot(tri, f_mid, preferred_element_type=F32)
           + jnp.dot(tri, f_lo, preferred_element_type=F32)) * LOG2_E
    n_sub = rows // SUB
    last = [cum[(a + 1) * SUB - 1:(a + 1) * SUB] for a in range(n_sub)]
    tot = jnp.concatenate([jnp.broadcast_to(r, (SUB, w)) for r in last], axis=0)
    k_dec = (key * jnp.exp2(tot - cum)).astype(BF16)
    q_dec = (query * jnp.exp2(cum)).astype(BF16)
    value_b = value.astype(BF16)

    head_ones = hones_ref[...]
    head_mask = hmask_ref[...]
    sub_row = lax.broadcasted_iota(jnp.int32, (SUB, w), 0)
    half_row = lax.broadcasted_iota(jnp.int32, (SUB // 2, w), 0) + SUB // 2

    st = st_scr[b]
    outs = []
    for a in range(n_sub):
        rs = slice(a * SUB, (a + 1) * SUB)
        o_inter = lax.dot_general(q_dec[rs], st.astype(BF16), (((1,), (1,)), ((), ())),
                                  preferred_element_type=F32)
        cum_a, q_a, k_a, v_a = cum[rs], query[rs], key[rs], value[rs]
        pieces = []
        for s in range(SUB):
            if s < SUB // 2:
                decay = jnp.exp2(jnp.minimum(cum_a - cum_a[s:s + 1], 0.0))
                pieces.append(jnp.where(sub_row >= s, decay * (q_a * k_a[s:s + 1]), 0.0))
            else:
                decay = jnp.exp2(jnp.minimum(cum_a[SUB // 2:] - cum_a[s:s + 1], 0.0))
                pieces.append(jnp.where(half_row >= s, decay * (q_a[SUB // 2:] * k_a[s:s + 1]), 0.0))
        e_all = jnp.concatenate(pieces, axis=0).astype(BF16)
        scores = jnp.dot(e_all, head_ones, preferred_element_type=F32)
        o_top = jnp.zeros((SUB // 2, w), F32)
        o_bot = jnp.zeros((SUB // 2, w), F32)
        for s in range(SUB):
            if s < SUB // 2:
                o_top = o_top + scores[s * SUB:s * SUB + SUB // 2] * v_a[s:s + 1]
                o_bot = o_bot + scores[s * SUB + SUB // 2:(s + 1) * SUB] * v_a[s:s + 1]
            else:
                base = (SUB // 2) * SUB + (s - SUB // 2) * (SUB // 2)
                o_bot = o_bot + scores[base:base + SUB // 2] * v_a[s:s + 1]
        outs.append(o_inter + jnp.concatenate([o_top, o_bot], axis=0))
        upd_t = lax.dot_general(value_b[rs], k_dec[rs], (((0,), (0,)), ((), ())),
                                preferred_element_type=F32)
        st = jnp.exp2(last[a]) * st + upd_t * head_mask
    st_scr[b] = st

    o = jnp.concatenate(outs, axis=0)
    sq = o * o
    sq_hi = sq.astype(BF16)
    sq_lo = (sq - sq_hi.astype(F32)).astype(BF16)
    ms = (jnp.dot(sq_hi, head_ones, preferred_element_type=F32)
          + jnp.dot(sq_lo, head_ones, preferred_element_type=F32)) * (1.0 / HGRN_HEAD_DIM)
    y_hgrn = o * lax.rsqrt(ms + EPS) * onorm_ref[...] * _silu(og)
    y_ref[b, :, 0:POOL_WIDTH] = y_pool.astype(BF16)
    y_ref[b, :, POOL_WIDTH:] = y_hgrn.astype(BF16)


def _mix_constants():
    r = np.arange(MIX_ROWS)
    tri = ((r[:, None] // SUB == r[None, :] // SUB) & (r[None, :] <= r[:, None])).astype(np.float32)
    c = np.arange(HGRN_WIDTH) // HGRN_HEAD_DIM
    same_head = (c[:, None] == c[None, :]).astype(np.float32)
    return jnp.asarray(tri, BF16), jnp.asarray(same_head, BF16), jnp.asarray(same_head, F32)


def _mix(rest, lb_row, onorm_row, poolw_bd, pscale_row):
    bsz, seq, _ = rest.shape
    tri, head_ones, head_mask = _mix_constants()
    return pl.pallas_call(
        _mix_body,
        out_shape=jax.ShapeDtypeStruct((bsz, seq, POOL_WIDTH + HGRN_WIDTH), BF16),
        grid=(seq // MIX_ROWS,),
        in_specs=[pl.BlockSpec((bsz, MIX_ROWS, REST_WIDTH), lambda t: (0, t, 0)),
                  _const_spec((1, HGRN_WIDTH)), _const_spec((1, HGRN_WIDTH)),
                  _const_spec((POOL_WIDTH, POOL_WIDTH)), _const_spec((1, POOL_WIDTH)),
                  _const_spec((MIX_ROWS, MIX_ROWS)), _const_spec((HGRN_WIDTH, HGRN_WIDTH)),
                  _const_spec((HGRN_WIDTH, HGRN_WIDTH))],
        out_specs=pl.BlockSpec((bsz, MIX_ROWS, POOL_WIDTH + HGRN_WIDTH), lambda t: (0, t, 0)),
        scratch_shapes=[pltpu.VMEM((bsz, HGRN_WIDTH, HGRN_WIDTH), F32),
                        pltpu.VMEM((bsz, POOL_HALO, POOL_WIDTH), F32)],
        compiler_params=pltpu.CompilerParams(dimension_semantics=("arbitrary",),
                                             vmem_limit_bytes=VMEM_LIMIT),
        name="pool_hgrn",
    )(rest, lb_row, onorm_row, poolw_bd, pscale_row, tri, head_ones, head_mask)


def _block_diag(blocks):
    g, c, e = blocks.shape
    out = jnp.zeros((g * c, g * e), blocks.dtype)
    for i in range(g):
        out = out.at[i * c:(i + 1) * c, i * e:(i + 1) * e].set(blocks[i])
    return out


def kernel(x, ffn1_norm, ffn1_w_gate, ffn1_w_up, ffn1_w_down, mix_norm, w_in, q_norm, k_norm,
           pool_w, pool_scale, hgrn_lb, hgrn_out_norm, w_out, ffn2_norm, ffn2_w_gate, ffn2_w_up,
           ffn2_w_down):
    bsz, seq, d = x.shape
    n = bsz * seq

    inv_freq = ROPE_THETA ** (-jnp.arange(HALF, dtype=F32) / HALF)
    ang = jnp.arange(seq).astype(F32)[:, None] * inv_freq[None, :]
    cos_t, sin_t = jnp.cos(ang).T, jnp.sin(ang).T

    lb_soft = jax.nn.softmax(hgrn_lb.astype(F32), axis=0)
    lower_bounds = jnp.concatenate([jnp.zeros_like(lb_soft[:1]), jnp.cumsum(lb_soft[:-1], axis=0)],
                                   axis=0)

    ffn1_f32 = (ffn1_w_gate, ffn1_w_up, ffn1_w_down)
    ffn2_f32 = (ffn2_w_gate, ffn2_w_up, ffn2_w_down)
    gains1, gains2 = ffn1_norm[:, None, :], ffn2_norm[:, None, :]
    w1 = tuple(w[0].astype(BF16) for w in ffn1_f32)
    w_in_b = w_in.astype(BF16)
    w_qkv_t = jnp.swapaxes(w_in_b[:, :, :3 * ATTN_WIDTH], 1, 2)
    w_rest = w_in_b[:, :, 3 * ATTN_WIDTH:]
    w_out_b = w_out.astype(BF16)

    x2d = x.reshape(n, d)
    for l in range(DEPTH):
        x2d, w2 = _ffn(x2d, gains1, l, w1, cast_from=ffn2_f32, cast_layer=l)
        qt, k, vt, rest = _proj(x2d.reshape(bsz, seq, d), mix_norm[l][None, :], w_qkv_t[l], w_rest[l],
                                q_norm[l][:, None], k_norm[l][:, None], cos_t, sin_t)
        y_attn = _attn(qt, k, vt)
        y_rest = _mix(rest, lower_bounds[l][None, :], jnp.tile(hgrn_out_norm[l], 4)[None, :],
                      _block_diag(pool_w[l]).astype(BF16), pool_scale[l][None, :])
        lead = (y_attn.reshape(n, ATTN_WIDTH), y_rest.reshape(n, ATTN_WIDTH))
        last = l == DEPTH - 1
        x2d, w1 = _ffn(x2d, gains2, l, w2, lead=lead, w_out=w_out_b,
                       cast_from=() if last else ffn1_f32, cast_layer=0 if last else l + 1)
    return x2d.reshape(bsz, seq, d)
```

```python
import functools

import jax
import jax.numpy as jnp
import numpy as np
from jax import lax
from jax.experimental import pallas as pl
from jax.experimental.pallas import tpu as pltpu

F32 = jnp.float32
BF16 = jnp.bfloat16

D_MODEL = 1024
DEPTH = 4
ATTN_HEADS = 8
HEAD_DIM = 64
HALF = HEAD_DIM // 2
ATTN_WIDTH = ATTN_HEADS * HEAD_DIM
HEAD_PAIRS = ATTN_HEADS // 2
PAIR_DIM = 2 * HEAD_DIM
POOL_WINDOWS = (2, 4, 8, 16)
POOL_GROUP_DIM = 64
POOL_WIDTH = 256
HGRN_HEAD_DIM = 64
HGRN_WIDTH = 256
REST_WIDTH = POOL_WIDTH + 4 * HGRN_WIDTH
D_FF = 2816
FF_CHUNK = 256
N_FF_CHUNKS = D_FF // FF_CHUNK
MOBA_BLOCK = 256
MOBA_TOPK = 3
ROPE_THETA = 10000.0
EPS = 1e-6
NEG_INF = -1e30
LB_FLOOR = 1e-20

FFN_ROWS = 1024
CAST_SLABS = 16
PROJ_ROWS = 512
MIX_ROWS = 128
SUB = 16
POOL_HALO = 16
ATTN_PAIRS = 4
ATTN_UNROLLS = (8, 4)
ONES_ROWS = 16
MASK_BIAS = 2 * NEG_INF
LOG2_E = 1.4426950408889634
VMEM_LIMIT = 56 * 1024 * 1024


def _rms_rows(x, gain_row):
    ms = jnp.mean(x * x, axis=-1, keepdims=True)
    return x * lax.rsqrt(ms + EPS) * gain_row


def _silu(x):
    return x * jax.nn.sigmoid(x)


def _swiglu_half_step(x, gain_row, wg_ref, wu_ref, wd_ref):
    h = _rms_rows(x, gain_row).astype(BF16)
    acc = jnp.zeros_like(x)
    for c in range(N_FF_CHUNKS):
        cols = slice(c * FF_CHUNK, (c + 1) * FF_CHUNK)
        g = jnp.dot(h, wg_ref[:, cols], preferred_element_type=F32)
        u = jnp.dot(h, wu_ref[:, cols], preferred_element_type=F32)
        acc = acc + jnp.dot((_silu(g) * u).astype(BF16), wd_ref[cols, :],
                            preferred_element_type=F32)
    return x + 0.5 * acc


def _ffn_body(*refs, n_lead, n_cast):
    x = refs[0][...]
    if n_lead:
        ya_ref, yr_ref, wo_ref = refs[1:4]
        x = x + jnp.dot(ya_ref[...], wo_ref[0, :ATTN_WIDTH, :], preferred_element_type=F32)
        x = x + jnp.dot(yr_ref[...], wo_ref[0, ATTN_WIDTH:, :], preferred_element_type=F32)
    gain_ref, wg_ref, wu_ref, wd_ref = refs[1 + n_lead:5 + n_lead]
    src = refs[5 + n_lead:5 + n_lead + n_cast]
    o_ref = refs[5 + n_lead + n_cast]
    dst = refs[6 + n_lead + n_cast:]
    o_ref[...] = _swiglu_half_step(x, gain_ref[0], wg_ref, wu_ref, wd_ref)
    for s_ref, d_ref in zip(src, dst):
        d_ref[...] = s_ref[0].astype(BF16)


def _const_spec(shape):
    return pl.BlockSpec(shape, lambda *_: (0,) * len(shape), pipeline_mode=pl.Buffered(1))


def _layer_spec(shape, layer):
    return pl.BlockSpec((1,) + shape, lambda *_: (layer,) + (0,) * len(shape),
                        pipeline_mode=pl.Buffered(1))


def _ffn(x2d, gains, layer, weights, lead=(), w_out=None, cast_from=(), cast_layer=0):
    n = x2d.shape[0]
    tile = FFN_ROWS
    steps = n // tile
    per_slab = max(steps // CAST_SLABS, 1)
    row_spec = pl.BlockSpec((tile, D_MODEL), lambda i: (i, 0))
    half_spec = pl.BlockSpec((tile, ATTN_WIDTH), lambda i: (i, 0))
    args, in_specs = [x2d], [row_spec]
    if lead:
        args += [*lead, w_out]
        in_specs += [half_spec, half_spec, _layer_spec((D_MODEL, D_MODEL), layer)]
    args += [gains, *weights]
    in_specs += [_layer_spec((1, D_MODEL), layer), _const_spec((D_MODEL, D_FF)),
                 _const_spec((D_MODEL, D_FF)), _const_spec((D_FF, D_MODEL))]
    out_shape = [jax.ShapeDtypeStruct((n, D_MODEL), F32)]
    out_specs = [row_spec]
    for w in cast_from:
        rows, cols = w.shape[1] * per_slab // steps, w.shape[2]
        args.append(w)
        in_specs.append(pl.BlockSpec((1, rows, cols), lambda i: (cast_layer, i // per_slab, 0)))
        out_shape.append(jax.ShapeDtypeStruct(w.shape[1:], BF16))
        out_specs.append(pl.BlockSpec((rows, cols), lambda i: (i // per_slab, 0)))
    outs = pl.pallas_call(
        functools.partial(_ffn_body, n_lead=3 if lead else 0, n_cast=len(cast_from)),
        out_shape=out_shape,
        grid=(steps,),
        in_specs=in_specs,
        out_specs=out_specs,
        compiler_params=pltpu.CompilerParams(dimension_semantics=("arbitrary",),
                                             vmem_limit_bytes=VMEM_LIMIT),
        name="outproj_ffn" if lead else "ffn",
    )(*args)
    return outs[0], tuple(outs[1:])


def _proj_body(x_ref, gain_ref, wt_ref, wr_ref, qn_ref, kn_ref, cos_ref, sin_ref,
               qt_ref, k_ref, vt_ref, rest_ref):
    h = _rms_rows(x_ref[0], gain_ref[...]).astype(BF16)
    rest_ref[0] = jnp.dot(h, wr_ref[...], preferred_element_type=F32)
    qkvt = lax.dot_general(wt_ref[...], h, (((1,), (1,)), ((), ())), preferred_element_type=F32)
    cos = cos_ref[...]
    sin = sin_ref[...]

    def norm_rope(t, gain_col):
        ms = jnp.mean(t * t, axis=0, keepdims=True)
        tn = t * lax.rsqrt(ms + EPS) * gain_col
        x1, x2 = tn[:HALF], tn[HALF:]
        return jnp.concatenate([x1 * cos - x2 * sin, x2 * cos + x1 * sin], axis=0)

    n_blk = PROJ_ROWS // MOBA_BLOCK
    for p in range(HEAD_PAIRS):
        lo = p * PAIR_DIM
        q_pair = jnp.concatenate(
            [norm_rope(qkvt[lo + e * HEAD_DIM:lo + (e + 1) * HEAD_DIM], qn_ref[...]) for e in range(2)],
            axis=0)
        k_pair = jnp.concatenate(
            [norm_rope(qkvt[ATTN_WIDTH + lo + e * HEAD_DIM:ATTN_WIDTH + lo + (e + 1) * HEAD_DIM],
                       kn_ref[...]) for e in range(2)], axis=0)
        v_pair = qkvt[2 * ATTN_WIDTH + lo:2 * ATTN_WIDTH + lo + PAIR_DIM].astype(BF16)
        k_ref[0, :, lo:lo + PAIR_DIM] = k_pair.T.astype(BF16)
        for b in range(n_blk):
            qt_ref[0, p, b] = q_pair[:, b * MOBA_BLOCK:(b + 1) * MOBA_BLOCK]
            vt_ref[0, p, b] = v_pair[:, b * MOBA_BLOCK:(b + 1) * MOBA_BLOCK]


def _proj(x3d, gain_row, wt, wr, qn_col, kn_col, cos_t, sin_t):
    bsz, seq, _ = x3d.shape
    nb = seq // MOBA_BLOCK
    n_blk = PROJ_ROWS // MOBA_BLOCK
    blk_spec = pl.BlockSpec((1, HEAD_PAIRS, n_blk, PAIR_DIM, MOBA_BLOCK), lambda b, i: (b, 0, i, 0, 0))
    return pl.pallas_call(
        _proj_body,
        out_shape=(jax.ShapeDtypeStruct((bsz, HEAD_PAIRS, nb, PAIR_DIM, MOBA_BLOCK), F32),
                   jax.ShapeDtypeStruct((bsz, seq, ATTN_WIDTH), BF16),
                   jax.ShapeDtypeStruct((bsz, HEAD_PAIRS, nb, PAIR_DIM, MOBA_BLOCK), BF16),
                   jax.ShapeDtypeStruct((bsz, seq, REST_WIDTH), F32)),
        grid=(bsz, seq // PROJ_ROWS),
        in_specs=[pl.BlockSpec((1, PROJ_ROWS, D_MODEL), lambda b, i: (b, i, 0)),
                  _const_spec((1, D_MODEL)),
                  _const_spec((3 * ATTN_WIDTH, D_MODEL)),
                  _const_spec((D_MODEL, REST_WIDTH)),
                  _const_spec((HEAD_DIM, 1)), _const_spec((HEAD_DIM, 1)),
                  pl.BlockSpec((HALF, PROJ_ROWS), lambda b, i: (0, i)),
                  pl.BlockSpec((HALF, PROJ_ROWS), lambda b, i: (0, i))],
        out_specs=(blk_spec,
                   pl.BlockSpec((1, PROJ_ROWS, ATTN_WIDTH), lambda b, i: (b, i, 0)),
                   blk_spec,
                   pl.BlockSpec((1, PROJ_ROWS, REST_WIDTH), lambda b, i: (b, i, 0))),
        compiler_params=pltpu.CompilerParams(dimension_semantics=("arbitrary", "arbitrary"),
                                             vmem_limit_bytes=VMEM_LIMIT),
        name="proj",
    )(x3d, gain_row, wt, wr, qn_col, kn_col, cos_t, sin_t)


def _attn_body(qt_ref, k_ref, vt_ref, y_ref, kmean_scr, bias_scr, *s_scrs, nb):
    i = pl.program_id(2)
    width = ATTN_PAIRS * PAIR_DIM

    @pl.when(i == 0)
    def _():
        for j in range(nb):
            kb = k_ref[0, j * MOBA_BLOCK:(j + 1) * MOBA_BLOCK, :].astype(F32)
            kmean_scr[j:j + 1, :] = jnp.sum(kb, axis=0, keepdims=True) * (1.0 / MOBA_BLOCK)

    kmean = kmean_scr[...]
    lane_p = lax.broadcasted_iota(jnp.int32, (nb, PAIR_DIM), 1)
    qrow = lax.broadcasted_iota(jnp.int32, (PAIR_DIM, MOBA_BLOCK), 0)
    blk = lax.broadcasted_iota(jnp.int32, (nb, MOBA_BLOCK), 0)
    valid = blk < i

    heads = [(p, e) for p in range(ATTN_PAIRS) for e in range(2)]

    def split(v):
        hi = v.astype(BF16)
        return hi, (v - hi.astype(F32)).astype(BF16)

    gates = []
    for p in range(ATTN_PAIRS):
        q_hi, q_lo = split(qt_ref[0, p, 0])
        km_pair = kmean[:, p * PAIR_DIM:(p + 1) * PAIR_DIM]
        km_hi, km_lo = split(jnp.concatenate(
            [jnp.where((lane_p >= e * HEAD_DIM) & (lane_p < (e + 1) * HEAD_DIM), km_pair, 0.0)
             for e in range(2)], axis=0))
        gates.append(jnp.dot(km_hi, q_hi, preferred_element_type=F32)
                     + jnp.dot(km_lo, q_hi, preferred_element_type=F32)
                     + jnp.dot(km_hi, q_lo, preferred_element_type=F32))

    q_heads = []
    for h, (p, e) in enumerate(heads):
        qt = qt_ref[0, p, 0]
        gate = gates[p][e * nb:(e + 1) * nb]
        sel = jnp.zeros((nb, MOBA_BLOCK), F32)
        for _ in range(MOBA_TOPK):
            cand = jnp.where(valid & (sel == 0.0), gate, -jnp.inf)
            best = jnp.max(cand, axis=0, keepdims=True)
            is_best = (cand == best) & (best > -jnp.inf)
            first = jnp.min(jnp.where(is_best, blk, nb), axis=0, keepdims=True)
            sel = jnp.where(blk == first, 1.0, sel)
        bias_scr[h] = jnp.where(sel > 0.0, 0.0, MASK_BIAS)
        in_rows = (qrow >= e * HEAD_DIM) & (qrow < (e + 1) * HEAD_DIM)
        q_heads.append(jnp.where(in_rows, qt * (HEAD_DIM ** -0.5 * LOG2_E), 0.0).astype(BF16))

    ones_rows = jnp.ones((ONES_ROWS, MOBA_BLOCK), BF16)

    def values_ext(j, h):
        p, e = heads[h]
        return jnp.concatenate([vt_ref[0, p, j][e * HEAD_DIM:(e + 1) * HEAD_DIM, :], ones_rows], axis=0)

    def scores(j, h):
        p = heads[h][0]
        kj = k_ref[0, pl.ds(pl.multiple_of(j * MOBA_BLOCK, MOBA_BLOCK), MOBA_BLOCK),
                   p * PAIR_DIM:(p + 1) * PAIR_DIM]
        return jnp.dot(kj, q_heads[h], preferred_element_type=F32)

    n_heads = len(heads)
    acc_scrs = s_scrs[n_heads:]
    init = []
    for h in range(n_heads):
        s = scores(0, h)
        s_scrs[h][0] = s
        acc_scrs[h][...] = jnp.zeros((HEAD_DIM + ONES_ROWS, MOBA_BLOCK), F32)
        init += [jnp.full((1, MOBA_BLOCK), NEG_INF, F32), jnp.max(s, axis=0, keepdims=True)]

    def one_block(j, slot, carry):
        out = []
        for h in range(n_heads):
            m, cmax = carry[2 * h:2 * h + 2]
            brow = bias_scr[h, pl.ds(j, 1), :]
            m_new = jnp.maximum(m, cmax + brow)
            p = jnp.exp2(s_scrs[h][slot] + (brow - m_new)).astype(BF16)
            acc_scrs[h][...] = (jnp.exp2(m - m_new) * acc_scrs[h][...]
                                + jnp.dot(values_ext(j, h), p, preferred_element_type=F32))
            s_next = scores(j + 1, h)
            s_scrs[h][1 - slot] = s_next
            out += [m_new, jnp.max(s_next, axis=0, keepdims=True)]
        return tuple(out)

    done = 0
    carry = tuple(init)
    for unroll in ATTN_UNROLLS:
        shift = unroll.bit_length() - 1
        trips = lax.shift_right_logical(i - done, shift)

        def body(t, c, unroll=unroll, done=done):
            for u in range(unroll):
                c = one_block(done + unroll * t + u, u & 1, c)
            return c

        carry = lax.fori_loop(0, trips, body, carry)
        done = done + trips * unroll
    fin = lax.fori_loop(done, i, lambda j, c: one_block(j, j & 1, c), carry)

    key_pos = lax.broadcasted_iota(jnp.int32, (MOBA_BLOCK, MOBA_BLOCK), 0)
    q_pos = lax.broadcasted_iota(jnp.int32, (MOBA_BLOCK, MOBA_BLOCK), 1)
    causal_bias = jnp.where(key_pos <= q_pos, 0.0, MASK_BIAS)
    outs = []
    for h in range(n_heads):
        m = fin[2 * h]
        s = s_scrs[h][i & 1] + causal_bias
        m_new = jnp.maximum(m, jnp.max(s, axis=0, keepdims=True))
        p = jnp.exp2(s - m_new).astype(BF16)
        acc = (jnp.exp2(m - m_new) * acc_scrs[h][...]
               + jnp.dot(values_ext(i, h), p, preferred_element_type=F32))
        outs.append(acc[:HEAD_DIM] / acc[HEAD_DIM:HEAD_DIM + 1])
    for p in range(ATTN_PAIRS):
        o_pair = jnp.concatenate(outs[2 * p:2 * p + 2], axis=0)
        y_ref[0, :, p * PAIR_DIM:(p + 1) * PAIR_DIM] = o_pair.T.astype(BF16)


def _attn(qt, k, vt):
    bsz, _, nb, _, _ = qt.shape
    assert all(u % 2 == 0 and u & (u - 1) == 0 for u in ATTN_UNROLLS)
    seq = nb * MOBA_BLOCK
    width = ATTN_PAIRS * PAIR_DIM
    n_heads = 2 * ATTN_PAIRS
    return pl.pallas_call(
        functools.partial(_attn_body, nb=nb),
        out_shape=jax.ShapeDtypeStruct((bsz, seq, ATTN_WIDTH), BF16),
        grid=(bsz, HEAD_PAIRS // ATTN_PAIRS, nb),
        in_specs=[pl.BlockSpec((1, ATTN_PAIRS, 1, PAIR_DIM, MOBA_BLOCK), lambda b, p, i: (b, p, i, 0, 0)),
                  pl.BlockSpec((1, seq, width), lambda b, p, i: (b, 0, p)),
                  pl.BlockSpec((1, ATTN_PAIRS, nb, PAIR_DIM, MOBA_BLOCK),
                               lambda b, p, i: (b, p, 0, 0, 0))],
        out_specs=pl.BlockSpec((1, MOBA_BLOCK, width), lambda b, p, i: (b, i, p)),
        scratch_shapes=[pltpu.VMEM((nb, width), F32),
                        pltpu.VMEM((n_heads, nb, MOBA_BLOCK), F32)]
        + [pltpu.VMEM((2, MOBA_BLOCK, MOBA_BLOCK), F32) for _ in range(n_heads)]
        + [pltpu.VMEM((HEAD_DIM + ONES_ROWS, MOBA_BLOCK), F32) for _ in range(n_heads)],
        compiler_params=pltpu.CompilerParams(
            dimension_semantics=("arbitrary", "arbitrary", "arbitrary"),
            vmem_limit_bytes=VMEM_LIMIT),
        name="moba_attn",
    )(qt, k, vt)


def _mix_body(r_ref, *rest):
    st_scr, uprev_scr = rest[-2:]
    t = pl.program_id(0)

    @pl.when(t == 0)
    def _():
        st_scr[...] = jnp.zeros_like(st_scr)
        uprev_scr[...] = jnp.zeros_like(uprev_scr)

    for b in range(r_ref.shape[0]):
        _mix_one(b, t, r_ref, *rest)


def _mix_one(b, t, r_ref, lb_ref, onorm_ref, poolw_ref, pscale_ref, tri_ref, hones_ref, hmask_ref,
             y_ref, st_scr, uprev_scr):
    rows, w = MIX_ROWS, HGRN_WIDTH
    u = r_ref[b, :, 0:POOL_WIDTH]
    qh = r_ref[b, :, POOL_WIDTH:POOL_WIDTH + w]
    z = r_ref[b, :, POOL_WIDTH + w:POOL_WIDTH + 2 * w]
    value = r_ref[b, :, POOL_WIDTH + 2 * w:POOL_WIDTH + 3 * w]
    og = r_ref[b, :, POOL_WIDTH + 3 * w:POOL_WIDTH + 4 * w]

    lane = lax.broadcasted_iota(jnp.int32, (rows, w), 1)
    row = lax.broadcasted_iota(jnp.int32, (rows, w), 0)

    uext = jnp.concatenate([uprev_scr[b], u], axis=0)
    sums = []
    acc = uext
    for shift in (1, 2, 4, 8):
        acc = acc + pltpu.roll(acc, shift, 0)
        sums.append(acc[POOL_HALO:])
    in_g0, in_g01, in_g012 = (lane < g * POOL_GROUP_DIM for g in (1, 2, 3))
    win_sum = jnp.where(in_g0, sums[0], jnp.where(in_g01, sums[1], jnp.where(in_g012, sums[2], sums[3])))
    window = jnp.where(in_g0, POOL_WINDOWS[0],
                       jnp.where(in_g01, POOL_WINDOWS[1],
                                 jnp.where(in_g012, POOL_WINDOWS[2], POOL_WINDOWS[3])))
    count = jnp.minimum(t * rows + row + 1, window).astype(F32)
    diff = win_sum / count - u
    y_pool = jnp.dot(diff.astype(BF16), poolw_ref[...], preferred_element_type=F32) * pscale_ref[...]
    uprev_scr[b] = u[rows - POOL_HALO:]

    lb = lb_ref[...]
    log_sig = jnp.minimum(z, 0.0) - jnp.log1p(jnp.exp(-jnp.abs(z)))
    a_term = jnp.log(jnp.maximum(lb, LB_FLOOR))
    b_term = jnp.log1p(-lb) + log_sig
    log_f = jnp.maximum(a_term, b_term) + jnp.log1p(jnp.exp(-jnp.abs(a_term - b_term)))
    key = (1.0 - lb) * jax.nn.sigmoid(-z)
    query = _silu(qh) * (HGRN_HEAD_DIM ** -0.5)

    tri = tri_ref[...]
    f_hi = log_f.astype(BF16)
    f_rest = log_f - f_hi.astype(F32)
    f_mid = f_rest.astype(BF16)
    f_lo = (f_rest - f_mid.astype(F32)).astype(BF16)
    cum = (jnp.dot(tri, f_hi, preferred_element_type=F32) + jnp.dot(tri, f_mid, preferred_element_type=F32)
           + jnp.dot(tri, f_lo, preferred_element_type=F32)) * LOG2_E
    n_sub = rows // SUB
    last = [cum[(a + 1) * SUB - 1:(a + 1) * SUB] for a in range(n_sub)]
    tot = jnp.concatenate([jnp.broadcast_to(r, (SUB, w)) for r in last], axis=0)
    k_dec = (key * jnp.exp2(tot - cum)).astype(BF16)
    q_dec = (query * jnp.exp2(cum)).astype(BF16)
    value_b = value.astype(BF16)

    head_ones = hones_ref[...]
    head_mask = hmask_ref[...]
    sub_row = lax.broadcasted_iota(jnp.int32, (SUB, w), 0)
    half_row = lax.broadcasted_iota(jnp.int32, (SUB // 2, w), 0) + SUB // 2

    st = st_scr[b]
    outs = []
    for a in range(n_sub):
        rs = slice(a * SUB, (a + 1) * SUB)
        o_inter = lax.dot_general(q_dec[rs], st.astype(BF16), (((1,), (1,)), ((), ())),
                                  preferred_element_type=F32)
        cum_a, q_a, k_a, v_a = cum[rs], query[rs], key[rs], value[rs]
        pieces = []
        for s in range(SUB):
            if s < SUB // 2:
                decay = jnp.exp2(jnp.minimum(cum_a - cum_a[s:s + 1], 0.0))
                pieces.append(jnp.where(sub_row >= s, decay * (q_a * k_a[s:s + 1]), 0.0))
            else:
                decay = jnp.exp2(jnp.minimum(cum_a[SUB // 2:] - cum_a[s:s + 1], 0.0))
                pieces.append(jnp.where(half_row >= s, decay * (q_a[SUB // 2:] * k_a[s:s + 1]), 0.0))
        e_all = jnp.concatenate(pieces, axis=0).astype(BF16)
        scores = jnp.dot(e_all, head_ones, preferred_element_type=F32)
        o_top = jnp.zeros((SUB // 2, w), F32)
        o_bot = jnp.zeros((SUB // 2, w), F32)
        for s in range(SUB):
            if s < SUB // 2:
                o_top = o_top + scores[s * SUB:s * SUB + SUB // 2] * v_a[s:s + 1]
                o_bot = o_bot + scores[s * SUB + SUB // 2:(s + 1) * SUB] * v_a[s:s + 1]
            else:
                base = (SUB // 2) * SUB + (s - SUB // 2) * (SUB // 2)
                o_bot = o_bot + scores[base:base + SUB // 2] * v_a[s:s + 1]
        outs.append(o_inter + jnp.concatenate([o_top, o_bot], axis=0))
        upd_t = lax.dot_general(value_b[rs], k_dec[rs], (((0,), (0,)), ((), ())),
                                preferred_element_type=F32)
        st = jnp.exp2(last[a]) * st + upd_t * head_mask
    st_scr[b] = st

    o = jnp.concatenate(outs, axis=0)
    sq = o * o
    sq_hi = sq.astype(BF16)
    sq_lo = (sq - sq_hi.astype(F32)).astype(BF16)
    ms = (jnp.dot(sq_hi, head_ones, preferred_element_type=F32)
          + jnp.dot(sq_lo, head_ones, preferred_element_type=F32)) * (1.0 / HGRN_HEAD_DIM)
    y_hgrn = o * lax.rsqrt(ms + EPS) * onorm_ref[...] * _silu(og)
    y_ref[b, :, 0:POOL_WIDTH] = y_pool.astype(BF16)
    y_ref[b, :, POOL_WIDTH:] = y_hgrn.astype(BF16)


def _mix_constants():
    r = np.arange(MIX_ROWS)
    tri = ((r[:, None] // SUB == r[None, :] // SUB) & (r[None, :] <= r[:, None])).astype(np.float32)
    c = np.arange(HGRN_WIDTH) // HGRN_HEAD_DIM
    same_head = (c[:, None] == c[None, :]).astype(np.float32)
    return jnp.asarray(tri, BF16), jnp.asarray(same_head, BF16), jnp.asarray(same_head, F32)


def _mix(rest, lb_row, onorm_row, poolw_bd, pscale_row):
    bsz, seq, _ = rest.shape
    tri, head_ones, head_mask = _mix_constants()
    return pl.pallas_call(
        _mix_body,
        out_shape=jax.ShapeDtypeStruct((bsz, seq, POOL_WIDTH + HGRN_WIDTH), BF16),
        grid=(seq // MIX_ROWS,),
        in_specs=[pl.BlockSpec((bsz, MIX_ROWS, REST_WIDTH), lambda t: (0, t, 0)),
                  _const_spec((1, HGRN_WIDTH)), _const_spec((1, HGRN_WIDTH)),
                  _const_spec((POOL_WIDTH, POOL_WIDTH)), _const_spec((1, POOL_WIDTH)),
                  _const_spec((MIX_ROWS, MIX_ROWS)), _const_spec((HGRN_WIDTH, HGRN_WIDTH)),
                  _const_spec((HGRN_WIDTH, HGRN_WIDTH))],
        out_specs=pl.BlockSpec((bsz, MIX_ROWS, POOL_WIDTH + HGRN_WIDTH), lambda t: (0, t, 0)),
        scratch_shapes=[pltpu.VMEM((bsz, HGRN_WIDTH, HGRN_WIDTH), F32),
                        pltpu.VMEM((bsz, POOL_HALO, POOL_WIDTH), F32)],
        compiler_params=pltpu.CompilerParams(dimension_semantics=("arbitrary",),
                                             vmem_limit_bytes=VMEM_LIMIT),
        name="pool_hgrn",
    )(rest, lb_row, onorm_row, poolw_bd, pscale_row, tri, head_ones, head_mask)


def _block_diag(blocks):
    g, c, e = blocks.shape
    out = jnp.zeros((g * c, g * e), blocks.dtype)
    for i in range(g):
        out = out.at[i * c:(i + 1) * c, i * e:(i + 1) * e].set(blocks[i])
    return out


def kernel(x, ffn1_norm, ffn1_w_gate, ffn1_w_up, ffn1_w_down, mix_norm, w_in, q_norm, k_norm,
           pool_w, pool_scale, hgrn_lb, hgrn_out_norm, w_out, ffn2_norm, ffn2_w_gate, ffn2_w_up,
           ffn2_w_down):
    bsz, seq, d = x.shape
    n = bsz * seq

    inv_freq = ROPE_THETA ** (-jnp.arange(HALF, dtype=F32) / HALF)
    ang = jnp.arange(seq).astype(F32)[:, None] * inv_freq[None, :]
    cos_t, sin_t = jnp.cos(ang).T, jnp.sin(ang).T

    lb_soft = jax.nn.softmax(hgrn_lb.astype(F32), axis=0)
    lower_bounds = jnp.concatenate([jnp.zeros_like(lb_soft[:1]), jnp.cumsum(lb_soft[:-1], axis=0)],
                                   axis=0)

    ffn1_f32 = (ffn1_w_gate, ffn1_w_up, ffn1_w_down)
    ffn2_f32 = (ffn2_w_gate, ffn2_w_up, ffn2_w_down)
    gains1, gains2 = ffn1_norm[:, None, :], ffn2_norm[:, None, :]
    w1 = tuple(w[0].astype(BF16) for w in ffn1_f32)
    w_in_b = w_in.astype(BF16)
    w_qkv_t = jnp.swapaxes(w_in_b[:, :, :3 * ATTN_WIDTH], 1, 2)
    w_rest = w_in_b[:, :, 3 * ATTN_WIDTH:]
    w_out_b = w_out.astype(BF16)

    x2d = x.reshape(n, d)
    for l in range(DEPTH):
        x2d, w2 = _ffn(x2d, gains1, l, w1, cast_from=ffn2_f32, cast_layer=l)
        qt, k, vt, rest = _proj(x2d.reshape(bsz, seq, d), mix_norm[l][None, :], w_qkv_t[l], w_rest[l],
                                q_norm[l][:, None], k_norm[l][:, None], cos_t, sin_t)
        y_attn = _attn(qt, k, vt)
        y_rest = _mix(rest, lower_bounds[l][None, :], jnp.tile(hgrn_out_norm[l], 4)[None, :],
                      _block_diag(pool_w[l]).astype(BF16), pool_scale[l][None, :])
        lead = (y_attn.reshape(n, ATTN_WIDTH), y_rest.reshape(n, ATTN_WIDTH))
        last = l == DEPTH - 1
        x2d, w1 = _ffn(x2d, gains2, l, w2, lead=lead, w_out=w_out_b,
                       cast_from=() if last else ffn1_f32, cast_layer=0 if last else l + 1)
    return x2d.reshape(bsz, seq, d)
```

```python
import functools

import jax
import jax.numpy as jnp
import numpy as np
from jax import lax
from jax.experimental import pallas as pl
from jax.experimental.pallas import tpu as pltpu

F32 = jnp.float32
BF16 = jnp.bfloat16

D_MODEL = 1024
DEPTH = 4
ATTN_HEADS = 8
HEAD_DIM = 64
HALF = HEAD_DIM // 2
ATTN_WIDTH = ATTN_HEADS * HEAD_DIM
HEAD_PAIRS = ATTN_HEADS // 2
PAIR_DIM = 2 * HEAD_DIM
POOL_WINDOWS = (2, 4, 8, 16)
POOL_GROUP_DIM = 64
POOL_WIDTH = 256
HGRN_HEAD_DIM = 64
HGRN_WIDTH = 256
REST_WIDTH = POOL_WIDTH + 4 * HGRN_WIDTH
D_FF = 2816
FF_CHUNK = 256
N_FF_CHUNKS = D_FF // FF_CHUNK
MOBA_BLOCK = 256
MOBA_TOPK = 3
ROPE_THETA = 10000.0
EPS = 1e-6
NEG_INF = -1e30
LB_FLOOR = 1e-20

FFN_ROWS = 1024
CAST_SLABS = 16
PROJ_ROWS = 512
MIX_ROWS = 128
SUB = 16
POOL_HALO = 16
ATTN_PAIRS = 4
ATTN_UNROLLS = (8, 4)
ONES_ROWS = 16
MASK_BIAS = 2 * NEG_INF
LOG2_E = 1.4426950408889634
VMEM_LIMIT = 56 * 1024 * 1024


def _rms_rows(x, gain_row):
    ms = jnp.mean(x * x, axis=-1, keepdims=True)
    return x * lax.rsqrt(ms + EPS) * gain_row


def _silu(x):
    return x * jax.nn.sigmoid(x)


def _swiglu_half_step(x, gain_row, wg_ref, wu_ref, wd_ref):
    h = _rms_rows(x, gain_row).astype(BF16)
    acc = jnp.zeros_like(x)
    for c in range(N_FF_CHUNKS):
        cols = slice(c * FF_CHUNK, (c + 1) * FF_CHUNK)
        g = jnp.dot(h, wg_ref[:, cols], preferred_element_type=F32)
        u = jnp.dot(h, wu_ref[:, cols], preferred_element_type=F32)
        acc = acc + jnp.dot((_silu(g) * u).astype(BF16), wd_ref[cols, :],
                            preferred_element_type=F32)
    return x + 0.5 * acc


def _ffn_body(*refs, n_lead, n_cast):
    x = refs[0][...]
    if n_lead:
        ya_ref, yr_ref, wo_ref = refs[1:4]
        x = x + jnp.dot(ya_ref[...], wo_ref[0, :ATTN_WIDTH, :], preferred_element_type=F32)
        x = x + jnp.dot(yr_ref[...], wo_ref[0, ATTN_WIDTH:, :], preferred_element_type=F32)
    gain_ref, wg_ref, wu_ref, wd_ref = refs[1 + n_lead:5 + n_lead]
    src = refs[5 + n_lead:5 + n_lead + n_cast]
    o_ref = refs[5 + n_lead + n_cast]
    dst = refs[6 + n_lead + n_cast:]
    o_ref[...] = _swiglu_half_step(x, gain_ref[0], wg_ref, wu_ref, wd_ref)
    for s_ref, d_ref in zip(src, dst):
        d_ref[...] = s_ref[0].astype(BF16)


def _const_spec(shape):
    return pl.BlockSpec(shape, lambda *_: (0,) * len(shape), pipeline_mode=pl.Buffered(1))


def _layer_spec(shape, layer):
    return pl.BlockSpec((1,) + shape, lambda *_: (layer,) + (0,) * len(shape),
                        pipeline_mode=pl.Buffered(1))


def _ffn(x2d, gains, layer, weights, lead=(), w_out=None, cast_from=(), cast_layer=0):
    n = x2d.shape[0]
    tile = FFN_ROWS
    steps = n // tile
    per_slab = max(steps // CAST_SLABS, 1)
    row_spec = pl.BlockSpec((tile, D_MODEL), lambda i: (i, 0))
    half_spec = pl.BlockSpec((tile, ATTN_WIDTH), lambda i: (i, 0))
    args, in_specs = [x2d], [row_spec]
    if lead:
        args += [*lead, w_out]
        in_specs += [half_spec, half_spec, _layer_spec((D_MODEL, D_MODEL), layer)]
    args += [gains, *weights]
    in_specs += [_layer_spec((1, D_MODEL), layer), _const_spec((D_MODEL, D_FF)),
                 _const_spec((D_MODEL, D_FF)), _const_spec((D_FF, D_MODEL))]
    out_shape = [jax.ShapeDtypeStruct((n, D_MODEL), F32)]
    out_specs = [row_spec]
    for w in cast_from:
        rows, cols = w.shape[1] * per_slab // steps, w.shape[2]
        args.append(w)
        in_specs.append(pl.BlockSpec((1, rows, cols), lambda i: (cast_layer, i // per_slab, 0)))
        out_shape.append(jax.ShapeDtypeStruct(w.shape[1:], BF16))
        out_specs.append(pl.BlockSpec((rows, cols), lambda i: (i // per_slab, 0)))
    outs = pl.pallas_call(
        functools.partial(_ffn_body, n_lead=3 if lead else 0, n_cast=len(cast_from)),
        out_shape=out_shape,
        grid=(steps,),
        in_specs=in_specs,
        out_specs=out_specs,
        compiler_params=pltpu.CompilerParams(dimension_semantics=("arbitrary",),
                                             vmem_limit_bytes=VMEM_LIMIT),
        name="outproj_ffn" if lead else "ffn",
    )(*args)
    return outs[0], tuple(outs[1:])


def _proj_body(x_ref, gain_ref, wt_ref, wr_ref, qn_ref, kn_ref, cos_ref, sin_ref,
               qt_ref, k_ref, vt_ref, rest_ref):
    h = _rms_rows(x_ref[0], gain_ref[...]).astype(BF16)
    rest_ref[0] = jnp.dot(h, wr_ref[...], preferred_element_type=F32)
    qkvt = lax.dot_general(wt_ref[...], h, (((1,), (1,)), ((), ())), preferred_element_type=F32)
    cos = cos_ref[...]
    sin = sin_ref[...]

    def norm_rope(t, gain_col):
        ms = jnp.mean(t * t, axis=0, keepdims=True)
        tn = t * lax.rsqrt(ms + EPS) * gain_col
        x1, x2 = tn[:HALF], tn[HALF:]
        return jnp.concatenate([x1 * cos - x2 * sin, x2 * cos + x1 * sin], axis=0)

    n_blk = PROJ_ROWS // MOBA_BLOCK
    for p in range(HEAD_PAIRS):
        lo = p * PAIR_DIM
        q_pair = jnp.concatenate(
            [norm_rope(qkvt[lo + e * HEAD_DIM:lo + (e + 1) * HEAD_DIM], qn_ref[...]) for e in range(2)],
            axis=0)
        k_pair = jnp.concatenate(
            [norm_rope(qkvt[ATTN_WIDTH + lo + e * HEAD_DIM:ATTN_WIDTH + lo + (e + 1) * HEAD_DIM],
                       kn_ref[...]) for e in range(2)], axis=0)
        v_pair = qkvt[2 * ATTN_WIDTH + lo:2 * ATTN_WIDTH + lo + PAIR_DIM].astype(BF16)
        k_ref[0, :, lo:lo + PAIR_DIM] = k_pair.T.astype(BF16)
        for b in range(n_blk):
            qt_ref[0, p, b] = q_pair[:, b * MOBA_BLOCK:(b + 1) * MOBA_BLOCK]
            vt_ref[0, p, b] = v_pair[:, b * MOBA_BLOCK:(b + 1) * MOBA_BLOCK]


def _proj(x3d, gain_row, wt, wr, qn_col, kn_col, cos_t, sin_t):
    bsz, seq, _ = x3d.shape
    nb = seq // MOBA_BLOCK
    n_blk = PROJ_ROWS // MOBA_BLOCK
    blk_spec = pl.BlockSpec((1, HEAD_PAIRS, n_blk, PAIR_DIM, MOBA_BLOCK), lambda b, i: (b, 0, i, 0, 0))
    return pl.pallas_call(
        _proj_body,
        out_shape=(jax.ShapeDtypeStruct((bsz, HEAD_PAIRS, nb, PAIR_DIM, MOBA_BLOCK), F32),
                   jax.ShapeDtypeStruct((bsz, seq, ATTN_WIDTH), BF16),
                   jax.ShapeDtypeStruct((bsz, HEAD_PAIRS, nb, PAIR_DIM, MOBA_BLOCK), BF16),
                   jax.ShapeDtypeStruct((bsz, seq, REST_WIDTH), F32)),
        grid=(bsz, seq // PROJ_ROWS),
        in_specs=[pl.BlockSpec((1, PROJ_ROWS, D_MODEL), lambda b, i: (b, i, 0)),
                  _const_spec((1, D_MODEL)),
                  _const_spec((3 * ATTN_WIDTH, D_MODEL)),
                  _const_spec((D_MODEL, REST_WIDTH)),
                  _const_spec((HEAD_DIM, 1)), _const_spec((HEAD_DIM, 1)),
                  pl.BlockSpec((HALF, PROJ_ROWS), lambda b, i: (0, i)),
                  pl.BlockSpec((HALF, PROJ_ROWS), lambda b, i: (0, i))],
        out_specs=(blk_spec,
                   pl.BlockSpec((1, PROJ_ROWS, ATTN_WIDTH), lambda b, i: (b, i, 0)),
                   blk_spec,
                   pl.BlockSpec((1, PROJ_ROWS, REST_WIDTH), lambda b, i: (b, i, 0))),
        compiler_params=pltpu.CompilerParams(dimension_semantics=("arbitrary", "arbitrary"),
                                             vmem_limit_bytes=VMEM_LIMIT),
        name="proj",
    )(x3d, gain_row, wt, wr, qn_col, kn_col, cos_t, sin_t)


def _attn_body(qt_ref, k_ref, vt_ref, y_ref, kmean_scr, bias_scr, *s_scrs, nb):
    i = pl.program_id(2)
    width = ATTN_PAIRS * PAIR_DIM

    @pl.when(i == 0)
    def _():
        for j in range(nb):
            kb = k_ref[0, j * MOBA_BLOCK:(j + 1) * MOBA_BLOCK, :].astype(F32)
            kmean_scr[j:j + 1, :] = jnp.sum(kb, axis=0, keepdims=True) * (1.0 / MOBA_BLOCK)

    kmean = kmean_scr[...]
    lane_p = lax.broadcasted_iota(jnp.int32, (nb, PAIR_DIM), 1)
    qrow = lax.broadcasted_iota(jnp.int32, (PAIR_DIM, MOBA_BLOCK), 0)
    blk = lax.broadcasted_iota(jnp.int32, (nb, MOBA_BLOCK), 0)
    valid = blk < i

    heads = [(p, e) for p in range(ATTN_PAIRS) for e in range(2)]

    def split(v):
        hi = v.astype(BF16)
        return hi, (v - hi.astype(F32)).astype(BF16)

    gates = []
    for p in range(ATTN_PAIRS):
        q_hi, q_lo = split(qt_ref[0, p, 0])
        km_pair = kmean[:, p * PAIR_DIM:(p + 1) * PAIR_DIM]
        km_hi, km_lo = split(jnp.concatenate(
            [jnp.where((lane_p >= e * HEAD_DIM) & (lane_p < (e + 1) * HEAD_DIM), km_pair, 0.0)
             for e in range(2)], axis=0))
        gates.append(jnp.dot(km_hi, q_hi, preferred_element_type=F32)
                     + jnp.dot(km_lo, q_hi, preferred_element_type=F32)
                     + jnp.dot(km_hi, q_lo, preferred_element_type=F32))

    q_heads = []
    for h, (p, e) in enumerate(heads):
        qt = qt_ref[0, p, 0]
        gate = gates[p][e * nb:(e + 1) * nb]
        sel = jnp.zeros((nb, MOBA_BLOCK), F32)
        for _ in range(MOBA_TOPK):
            cand = jnp.where(valid & (sel == 0.0), gate, -jnp.inf)
            best = jnp.max(cand, axis=0, keepdims=True)
            is_best = (cand == best) & (best > -jnp.inf)
            first = jnp.min(jnp.where(is_best, blk, nb), axis=0, keepdims=True)
            sel = jnp.where(blk == first, 1.0, sel)
        bias_scr[h] = jnp.where(sel > 0.0, 0.0, MASK_BIAS)
        in_rows = (qrow >= e * HEAD_DIM) & (qrow < (e + 1) * HEAD_DIM)
        q_heads.append(jnp.where(in_rows, qt * (HEAD_DIM ** -0.5 * LOG2_E), 0.0).astype(BF16))

    ones_rows = jnp.ones((ONES_ROWS, MOBA_BLOCK), BF16)

    def values_ext(j, h):
        p, e = heads[h]
        return jnp.concatenate([vt_ref[0, p, j][e * HEAD_DIM:(e + 1) * HEAD_DIM, :], ones_rows], axis=0)

    def scores(j, h):
        p = heads[h][0]
        kj = k_ref[0, pl.ds(pl.multiple_of(j * MOBA_BLOCK, MOBA_BLOCK), MOBA_BLOCK),
                   p * PAIR_DIM:(p + 1) * PAIR_DIM]
        return jnp.dot(kj, q_heads[h], preferred_element_type=F32)

    n_heads = len(heads)
    acc_scrs = s_scrs[n_heads:]
    init = []
    for h in range(n_heads):
        s = scores(0, h)
        s_scrs[h][0] = s
        acc_scrs[h][...] = jnp.zeros((HEAD_DIM + ONES_ROWS, MOBA_BLOCK), F32)
        init += [jnp.full((1, MOBA_BLOCK), NEG_INF, F32), jnp.max(s, axis=0, keepdims=True)]

    def one_block(j, slot, carry):
        out = []
        for h in range(n_heads):
            m, cmax = carry[2 * h:2 * h + 2]
            brow = bias_scr[h, pl.ds(j, 1), :]
            m_new = jnp.maximum(m, cmax + brow)
            p = jnp.exp2(s_scrs[h][slot] + (brow - m_new)).astype(BF16)
            acc_scrs[h][...] = (jnp.exp2(m - m_new) * acc_scrs[h][...]
                                + jnp.dot(values_ext(j, h), p, preferred_element_type=F32))
            s_next = scores(j + 1, h)
            s_scrs[h][1 - slot] = s_next
            out += [m_new, jnp.max(s_next, axis=0, keepdims=True)]
        return tuple(out)

    done = 0
    carry = tuple(init)
    for unroll in ATTN_UNROLLS:
        shift = unroll.bit_length() - 1
        trips = lax.shift_right_logical(i - done, shift)

        def body(t, c, unroll=unroll, done=done):
            for u in range(unroll):
                c = one_block(done + unroll * t + u, u & 1, c)
            return c

        carry = lax.fori_loop(0, trips, body, carry)
        done = done + trips * unroll
    fin = lax.fori_loop(done, i, lambda j, c: one_block(j, j & 1, c), carry)

    key_pos = lax.broadcasted_iota(jnp.int32, (MOBA_BLOCK, MOBA_BLOCK), 0)
    q_pos = lax.broadcasted_iota(jnp.int32, (MOBA_BLOCK, MOBA_BLOCK), 1)
    causal_bias = jnp.where(key_pos <= q_pos, 0.0, MASK_BIAS)
    outs = []
    for h in range(n_heads):
        m = fin[2 * h]
        s = s_scrs[h][i & 1]
        hb = MOBA_BLOCK // 2
        top = s[:hb] + causal_bias[:hb]
        bot = s[hb:, hb:] + causal_bias[hb:, hb:]
        m_top = jnp.max(top, axis=0, keepdims=True)
        m_bot = jnp.max(bot, axis=0, keepdims=True)
        m_new = jnp.maximum(m, jnp.concatenate([m_top[:, :hb], jnp.maximum(m_top[:, hb:], m_bot)], axis=1))
        p_bot = jnp.concatenate([jnp.zeros((hb, hb), F32), jnp.exp2(bot - m_new[:, hb:])], axis=1)
        p = jnp.concatenate([jnp.exp2(top - m_new), p_bot], axis=0).astype(BF16)
        acc = (jnp.exp2(m - m_new) * acc_scrs[h][...]
               + jnp.dot(values_ext(i, h), p, preferred_element_type=F32))
        outs.append(acc[:HEAD_DIM] / acc[HEAD_DIM:HEAD_DIM + 1])
    for p in range(ATTN_PAIRS):
        o_pair = jnp.concatenate(outs[2 * p:2 * p + 2], axis=0)
        y_ref[0, :, p * PAIR_DIM:(p + 1) * PAIR_DIM] = o_pair.T.astype(BF16)


def _attn(qt, k, vt):
    bsz, _, nb, _, _ = qt.shape
    assert all(u % 2 == 0 and u & (u - 1) == 0 for u in ATTN_UNROLLS)
    seq = nb * MOBA_BLOCK
    width = ATTN_PAIRS * PAIR_DIM
    n_heads = 2 * ATTN_PAIRS
    return pl.pallas_call(
        functools.partial(_attn_body, nb=nb),
        out_shape=jax.ShapeDtypeStruct((bsz, seq, ATTN_WIDTH), BF16),
        grid=(bsz, HEAD_PAIRS // ATTN_PAIRS, nb),
        in_specs=[pl.BlockSpec((1, ATTN_PAIRS, 1, PAIR_DIM, MOBA_BLOCK), lambda b, p, i: (b, p, i, 0, 0)),
                  pl.BlockSpec((1, seq, width), lambda b, p, i: (b, 0, p)),
                  pl.BlockSpec((1, ATTN_PAIRS, nb, PAIR_DIM, MOBA_BLOCK),
                               lambda b, p, i: (b, p, 0, 0, 0))],
        out_specs=pl.BlockSpec((1, MOBA_BLOCK, width), lambda b, p, i: (b, i, p)),
        scratch_shapes=[pltpu.VMEM((nb, width), F32),
                        pltpu.VMEM((n_heads, nb, MOBA_BLOCK), F32)]
        + [pltpu.VMEM((2, MOBA_BLOCK, MOBA_BLOCK), F32) for _ in range(n_heads)]
        + [pltpu.VMEM((HEAD_DIM + ONES_ROWS, MOBA_BLOCK), F32) for _ in range(n_heads)],
        compiler_params=pltpu.CompilerParams(
            dimension_semantics=("arbitrary", "arbitrary", "arbitrary"),
            vmem_limit_bytes=VMEM_LIMIT),
        name="moba_attn",
    )(qt, k, vt)


def _mix_body(r_ref, *rest):
    st_scr, uprev_scr = rest[-2:]
    t = pl.program_id(0)

    @pl.when(t == 0)
    def _():
        st_scr[...] = jnp.zeros_like(st_scr)
        uprev_scr[...] = jnp.zeros_like(uprev_scr)

    for b in range(r_ref.shape[0]):
        _mix_one(b, t, r_ref, *rest)


def _mix_one(b, t, r_ref, lb_ref, onorm_ref, poolw_ref, pscale_ref, tri_ref, hones_ref, hmask_ref,
             y_ref, st_scr, uprev_scr):
    rows, w = MIX_ROWS, HGRN_WIDTH
    u = r_ref[b, :, 0:POOL_WIDTH]
    qh = r_ref[b, :, POOL_WIDTH:POOL_WIDTH + w]
    z = r_ref[b, :, POOL_WIDTH + w:POOL_WIDTH + 2 * w]
    value = r_ref[b, :, POOL_WIDTH + 2 * w:POOL_WIDTH + 3 * w]
    og = r_ref[b, :, POOL_WIDTH + 3 * w:POOL_WIDTH + 4 * w]

    lane = lax.broadcasted_iota(jnp.int32, (rows, w), 1)
    row = lax.broadcasted_iota(jnp.int32, (rows, w), 0)

    uext = jnp.concatenate([uprev_scr[b], u], axis=0)
    sums = []
    acc = uext
    for shift in (1, 2, 4, 8):
        acc = acc + pltpu.roll(acc, shift, 0)
        sums.append(acc[POOL_HALO:])
    in_g0, in_g01, in_g012 = (lane < g * POOL_GROUP_DIM for g in (1, 2, 3))
    win_sum = jnp.where(in_g0, sums[0], jnp.where(in_g01, sums[1], jnp.where(in_g012, sums[2], sums[3])))
    window = jnp.where(in_g0, POOL_WINDOWS[0],
                       jnp.where(in_g01, POOL_WINDOWS[1],
                                 jnp.where(in_g012, POOL_WINDOWS[2], POOL_WINDOWS[3])))
    count = jnp.minimum(t * rows + row + 1, window).astype(F32)
    diff = win_sum / count - u
    y_pool = jnp.dot(diff.astype(BF16), poolw_ref[...], preferred_element_type=F32) * pscale_ref[...]
    uprev_scr[b] = u[rows - POOL_HALO:]

    lb = lb_ref[...]
    log_sig = jnp.minimum(z, 0.0) - jnp.log1p(jnp.exp(-jnp.abs(z)))
    a_term = jnp.log(jnp.maximum(lb, LB_FLOOR))
    b_term = jnp.log1p(-lb) + log_sig
    log_f = jnp.maximum(a_term, b_term) + jnp.log1p(jnp.exp(-jnp.abs(a_term - b_term)))
    key = (1.0 - lb) * jax.nn.sigmoid(-z)
    query = _silu(qh) * (HGRN_HEAD_DIM ** -0.5)

    tri = tri_ref[...]
    f_hi = log_f.astype(BF16)
    f_rest = log_f - f_hi.astype(F32)
    f_mid = f_rest.astype(BF16)
    f_lo = (f_rest - f_mid.astype(F32)).astype(BF16)
    cum = (jnp.dot(tri, f_hi, preferred_element_type=F32) + jnp.dot(tri, f_mid, preferred_element_type=F32)
           + jnp.dot(tri, f_lo, preferred_element_type=F32)) * LOG2_E
    n_sub = rows // SUB
    last = [cum[(a + 1) * SUB - 1:(a + 1) * SUB] for a in range(n_sub)]
    tot = jnp.concatenate([jnp.broadcast_to(r, (SUB, w)) for r in last], axis=0)
    k_dec = (key * jnp.exp2(tot - cum)).astype(BF16)
    q_dec = (query * jnp.exp2(cum)).astype(BF16)
    value_b = value.astype(BF16)

    head_ones = hones_ref[...]
    head_mask = hmask_ref[...]
    sub_row = lax.broadcasted_iota(jnp.int32, (SUB, w), 0)
    half_row = lax.broadcasted_iota(jnp.int32, (SUB // 2, w), 0) + SUB // 2

    st = st_scr[b]
    outs = []
    for a in range(n_sub):
        rs = slice(a * SUB, (a + 1) * SUB)
        o_inter = lax.dot_general(q_dec[rs], st.astype(BF16), (((1,), (1,)), ((), ())),
                                  preferred_element_type=F32)
        cum_a, q_a, k_a, v_a = cum[rs], query[rs], key[rs], value[rs]
        pieces = []
        for s in range(SUB):
            if s < SUB // 2:
                decay = jnp.exp2(jnp.minimum(cum_a - cum_a[s:s + 1], 0.0))
                pieces.append(jnp.where(sub_row >= s, decay * (q_a * k_a[s:s + 1]), 0.0))
            else:
                decay = jnp.exp2(jnp.minimum(cum_a[SUB // 2:] - cum_a[s:s + 1], 0.0))
                pieces.append(jnp.where(half_row >= s, decay * (q_a[SUB // 2:] * k_a[s:s + 1]), 0.0))
        e_all = jnp.concatenate(pieces, axis=0).astype(BF16)
        scores = jnp.dot(e_all, head_ones, preferred_element_type=F32)
        o_top = jnp.zeros((SUB // 2, w), F32)
        o_bot = jnp.zeros((SUB // 2, w), F32)
        for s in range(SUB):
            if s < SUB // 2:
                o_top = o_top + scores[s * SUB:s * SUB + SUB // 2] * v_a[s:s + 1]
                o_bot = o_bot + scores[s * SUB + SUB // 2:(s + 1) * SUB] * v_a[s:s + 1]
            else:
                base = (SUB // 2) * SUB + (s - SUB // 2) * (SUB // 2)
                o_bot = o_bot + scores[base:base + SUB // 2] * v_a[s:s + 1]
        outs.append(o_inter + jnp.concatenate([o_top, o_bot], axis=0))
        upd_t = lax.dot_general(value_b[rs], k_dec[rs], (((0,), (0,)), ((), ())),
                                preferred_element_type=F32)
        st = jnp.exp2(last[a]) * st + upd_t * head_mask
    st_scr[b] = st

    o = jnp.concatenate(outs, axis=0)
    sq = o * o
    sq_hi = sq.astype(BF16)
    sq_lo = (sq - sq_hi.astype(F32)).astype(BF16)
    ms = (jnp.dot(sq_hi, head_ones, preferred_element_type=F32)
          + jnp.dot(sq_lo, head_ones, preferred_element_type=F32)) * (1.0 / HGRN_HEAD_DIM)
    y_hgrn = o * lax.rsqrt(ms + EPS) * onorm_ref[...] * _silu(og)
    y_ref[b, :, 0:POOL_WIDTH] = y_pool.astype(BF16)
    y_ref[b, :, POOL_WIDTH:] = y_hgrn.astype(BF16)


def _mix_constants():
    r = np.arange(MIX_ROWS)
    tri = ((r[:, None] // SUB == r[None, :] // SUB) & (r[None, :] <= r[:, None])).astype(np.float32)
    c = np.arange(HGRN_WIDTH) // HGRN_HEAD_DIM
    same_head = (c[:, None] == c[None, :]).astype(np.float32)
    return jnp.asarray(tri, BF16), jnp.asarray(same_head, BF16), jnp.asarray(same_head, F32)


def _mix(rest, lb_row, onorm_row, poolw_bd, pscale_row):
    bsz, seq, _ = rest.shape
    tri, head_ones, head_mask = _mix_constants()
    return pl.pallas_call(
        _mix_body,
        out_shape=jax.ShapeDtypeStruct((bsz, seq, POOL_WIDTH + HGRN_WIDTH), BF16),
        grid=(seq // MIX_ROWS,),
        in_specs=[pl.BlockSpec((bsz, MIX_ROWS, REST_WIDTH), lambda t: (0, t, 0)),
                  _const_spec((1, HGRN_WIDTH)), _const_spec((1, HGRN_WIDTH)),
                  _const_spec((POOL_WIDTH, POOL_WIDTH)), _const_spec((1, POOL_WIDTH)),
                  _const_spec((MIX_ROWS, MIX_ROWS)), _const_spec((HGRN_WIDTH, HGRN_WIDTH)),
                  _const_spec((HGRN_WIDTH, HGRN_WIDTH))],
        out_specs=pl.BlockSpec((bsz, MIX_ROWS, POOL_WIDTH + HGRN_WIDTH), lambda t: (0, t, 0)),
        scratch_shapes=[pltpu.VMEM((bsz, HGRN_WIDTH, HGRN_WIDTH), F32),
                        pltpu.VMEM((bsz, POOL_HALO, POOL_WIDTH), F32)],
        compiler_params=pltpu.CompilerParams(dimension_semantics=("arbitrary",),
                                             vmem_limit_bytes=VMEM_LIMIT),
        name="pool_hgrn",
    )(rest, lb_row, onorm_row, poolw_bd, pscale_row, tri, head_ones, head_mask)


def _block_diag(blocks):
    g, c, e = blocks.shape
    out = jnp.zeros((g * c, g * e), blocks.dtype)
    for i in range(g):
        out = out.at[i * c:(i + 1) * c, i * e:(i + 1) * e].set(blocks[i])
    return out


def kernel(x, ffn1_norm, ffn1_w_gate, ffn1_w_up, ffn1_w_down, mix_norm, w_in, q_norm, k_norm,
           pool_w, pool_scale, hgrn_lb, hgrn_out_norm, w_out, ffn2_norm, ffn2_w_gate, ffn2_w_up,
           ffn2_w_down):
    bsz, seq, d = x.shape
    n = bsz * seq

    inv_freq = ROPE_THETA ** (-jnp.arange(HALF, dtype=F32) / HALF)
    ang = jnp.arange(seq).astype(F32)[:, None] * inv_freq[None, :]
    cos_t, sin_t = jnp.cos(ang).T, jnp.sin(ang).T

    lb_soft = jax.nn.softmax(hgrn_lb.astype(F32), axis=0)
    lower_bounds = jnp.concatenate([jnp.zeros_like(lb_soft[:1]), jnp.cumsum(lb_soft[:-1], axis=0)],
                                   axis=0)

    ffn1_f32 = (ffn1_w_gate, ffn1_w_up, ffn1_w_down)
    ffn2_f32 = (ffn2_w_gate, ffn2_w_up, ffn2_w_down)
    gains1, gains2 = ffn1_norm[:, None, :], ffn2_norm[:, None, :]
    w1 = tuple(w[0].astype(BF16) for w in ffn1_f32)
    w_in_b = w_in.astype(BF16)
    w_qkv_t = jnp.swapaxes(w_in_b[:, :, :3 * ATTN_WIDTH], 1, 2)
    w_rest = w_in_b[:, :, 3 * ATTN_WIDTH:]
    w_out_b = w_out.astype(BF16)

    x2d = x.reshape(n, d)
    for l in range(DEPTH):
        x2d, w2 = _ffn(x2d, gains1, l, w1, cast_from=ffn2_f32, cast_layer=l)
        qt, k, vt, rest = _proj(x2d.reshape(bsz, seq, d), mix_norm[l][None, :], w_qkv_t[l], w_rest[l],
                                q_norm[l][:, None], k_norm[l][:, None], cos_t, sin_t)
        y_attn = _attn(qt, k, vt)
        y_rest = _mix(rest, lower_bounds[l][None, :], jnp.tile(hgrn_out_norm[l], 4)[None, :],
                      _block_diag(pool_w[l]).astype(BF16), pool_scale[l][None, :])
        lead = (y_attn.reshape(n, ATTN_WIDTH), y_rest.reshape(n, ATTN_WIDTH))
        last = l == DEPTH - 1
        x2d, w1 = _ffn(x2d, gains2, l, w2, lead=lead, w_out=w_out_b,
                       cast_from=() if last else ffn1_f32, cast_layer=0 if last else l + 1)
    return x2d.reshape(bsz, seq, d)
```
